```python
import jax, jax.numpy as jnp
from jax import lax
import numpy as np


D_MODEL = 1024
BATCH = 4
SEQ = 8192
DEPTH = 1

N_MEM = 256
EPS = 1e-5
NEG = -1e30
BIG = 1e30

N_HEADS_A = 8
N_KV_A = 2
HG_A = N_HEADS_A // N_KV_A
DH_A = 64
CMP_BLOCK = 32
CMP_STRIDE = 16
SUB_PER_CMP = CMP_BLOCK // CMP_STRIDE
SLC_BLOCK = 64
SUB_PER_SLC = SLC_BLOCK // CMP_STRIDE
N_SEL = 16
WINDOW = 512
PHI_HIDDEN = 128
Q_BLOCK = 64

N_HEADS_B = 4
DH_B = 128
CONV_K = 4
CHUNK = 64

NSA_Q = N_HEADS_A * DH_A
NSA_KV = N_KV_A * DH_A
NSA_GATES = 3 * N_HEADS_A
M_INNER = N_HEADS_B * DH_B
IN_COLS = NSA_Q + 6 * NSA_KV + NSA_GATES + 2 * M_INNER + 2 * N_HEADS_B + 2 * D_MODEL

N_HEADS_X = 4
DH_X = D_MODEL // N_HEADS_X

N_EXPERTS = 32
TOP_K = 4
D_FF = D_MODEL
SWIGLU_LIMIT = 7.0
SWIGLU_ALPHA = 1.702
MOE_BLOCK = 256

kernel_name = 'hybrid_nsa_mlstm_moe_block'


def rmsnorm(x, g):
    xf = x.astype(jnp.float32)
    y = xf * lax.rsqrt(jnp.mean(xf * xf, axis=-1, keepdims=True) + EPS)
    return (y * g.astype(jnp.float32)).astype(x.dtype)


def nsa_mixer(q, k_cmp, v_cmp, k_slc, v_slc, k_win, v_win, gate_logits,
              pe_k, pe_v, w_phi1_k, w_phi2_k, w_phi1_v, w_phi2_v):
    B, T = q.shape[0], q.shape[1]
    f32 = jnp.float32
    scale = DH_A ** -0.5
    n_sub = T // CMP_STRIDE
    n_cmp = n_sub - SUB_PER_CMP + 1
    n_slc = T // SLC_BLOCK
    n_sel = min(N_SEL, n_slc)

    def compress(kv, pe, w1, w2):
        sub = kv.reshape(B, n_sub, CMP_STRIDE, N_KV_A, DH_A)
        blocks = jnp.concatenate([sub[:, r:r + n_cmp] for r in range(SUB_PER_CMP)], axis=2)
        blocks = blocks + pe[None, None, :, None, :]
        flat = blocks.transpose(0, 1, 3, 2, 4).reshape(B, n_cmp, N_KV_A, CMP_BLOCK * DH_A)
        return jax.nn.gelu(flat @ w1) @ w2

    kc = compress(k_cmp, pe_k, w_phi1_k, w_phi2_k)
    vc = compress(v_cmp, pe_v, w_phi1_v, w_phi2_v)
    cmp_end = jnp.arange(n_cmp) * CMP_STRIDE + (CMP_BLOCK - 1)
    kb = k_slc.reshape(B, n_slc, SLC_BLOCK, N_KV_A, DH_A).transpose(0, 3, 1, 2, 4)
    vb = v_slc.reshape(B, n_slc, SLC_BLOCK, N_KV_A, DH_A).transpose(0, 3, 1, 2, 4)
    kw = jnp.pad(k_win, ((0, 0), (WINDOW, 0), (0, 0), (0, 0)))
    vw = jnp.pad(v_win, ((0, 0), (WINDOW, 0), (0, 0), (0, 0)))
    b_idx = jnp.arange(B)[:, None, None, None]
    g_idx = jnp.arange(N_KV_A)[None, :, None, None]
    blk = jnp.arange(n_slc)
    in_slc = jnp.arange(SLC_BLOCK)
    in_win = jnp.arange(Q_BLOCK + WINDOW)

    def block_fn(i):
        q0 = i * Q_BLOCK
        qb = lax.dynamic_slice_in_dim(q, q0, Q_BLOCK, axis=1)
        t = q0 + jnp.arange(Q_BLOCK)
        s = jnp.einsum('bqghd,bjgd->bgqhj', qb, kc).astype(f32) * scale
        mc = (cmp_end[None, :] <= t[:, None])[None, None, :, None, :]
        p_cmp = jax.nn.softmax(jnp.where(mc, s, NEG), axis=-1) * mc
        o_cmp = jnp.einsum('bgqhj,bjgd->bqghd', p_cmp.astype(vc.dtype), vc)
        pg = jnp.pad(p_cmp.sum(axis=3), ((0, 0), (0, 0), (0, 0), (1, SUB_PER_SLC)))
        imp = sum(pg[..., 1 + m - n::SUB_PER_SLC][..., :n_slc]
                  for m in range(SUB_PER_SLC) for n in range(SUB_PER_CMP))
        cur = t // SLC_BLOCK
        valid = blk[None, :] <= cur[:, None]
        forced = (blk[None, :] == 0) | (blk[None, :] == cur[:, None]) | (blk[None, :] == cur[:, None] - 1)
        score = jnp.where(valid, jnp.where(forced, BIG, imp), NEG)
        _, idx = lax.top_k(score, n_sel)
        ks = kb[b_idx, g_idx, idx]
        vs = vb[b_idx, g_idx, idx]
        kpos = idx[..., None] * SLC_BLOCK + in_slc
        ms = (kpos <= t[None, None, :, None, None])[:, :, :, None]
        s = jnp.einsum('bqghd,bgqskd->bgqhsk', qb, ks).astype(f32) * scale
        s = jnp.where(ms, s, NEG).reshape(B, N_KV_A, Q_BLOCK, HG_A, n_sel * SLC_BLOCK)
        p = jax.nn.softmax(s, axis=-1).reshape(B, N_KV_A, Q_BLOCK, HG_A, n_sel, SLC_BLOCK)
        o_slc = jnp.einsum('bgqhsk,bgqskd->bqghd', p.astype(vs.dtype), vs)
        kwb = lax.dynamic_slice_in_dim(kw, q0, Q_BLOCK + WINDOW, axis=1)
        vwb = lax.dynamic_slice_in_dim(vw, q0, Q_BLOCK + WINDOW, axis=1)
        kp = q0 - WINDOW + in_win
        mw = ((kp[None, :] <= t[:, None]) & (kp[None, :] > t[:, None] - WINDOW)
              & (kp[None, :] >= 0))[None, None, :, None, :]
        s = jnp.einsum('bqghd,bkgd->bgqhk', qb, kwb).astype(f32) * scale
        p = jax.nn.softmax(jnp.where(mw, s, NEG), axis=-1)
        o_win = jnp.einsum('bgqhk,bkgd->bqghd', p.astype(vwb.dtype), vwb)
        return o_cmp, o_slc, o_win

    outs = lax.map(block_fn, jnp.arange(T // Q_BLOCK))

    def unblock(o):
        return o.transpose(1, 0, 2, 3, 4, 5).reshape(B, T, N_KV_A, HG_A, DH_A).astype(f32)

    o_cmp, o_slc, o_win = unblock(outs[0]), unblock(outs[1]), unblock(outs[2])
    g = jax.nn.sigmoid(gate_logits.astype(f32)).reshape(B, T, N_KV_A, HG_A, 3)
    o = g[..., 0, None] * o_cmp + g[..., 1, None] * o_slc + g[..., 2, None] * o_win
    return o.reshape(B, T, NSA_Q).astype(q.dtype)


def mlstm_mixer(x_m, o_pre, i_pre, f_pre, w_conv, b_conv, w_q, w_k, w_v, b_i, b_f, mh_gain, skip):
    B, T, _ = x_m.shape
    f32 = jnp.float32
    n_chunk = T // CHUNK
    xp = jnp.pad(x_m, ((0, 0), (CONV_K - 1, 0), (0, 0)))
    conv = sum(xp[:, r:r + T] * w_conv[r] for r in range(CONV_K)) + b_conv
    x_c = jax.nn.silu(conv)
    xc_h = x_c.reshape(B, T, N_HEADS_B, DH_B)
    xm_h = x_m.reshape(B, T, N_HEADS_B, DH_B)
    q = jnp.einsum('bthd,hde->bthe', xc_h, w_q).astype(f32)
    k = jnp.einsum('bthd,hde->bthe', xc_h, w_k).astype(f32) * (DH_B ** -0.5)
    v = jnp.einsum('bthd,hde->bthe', xm_h, w_v).astype(f32)
    log_f = jax.nn.log_sigmoid((f_pre + b_f).astype(f32))
    log_i = (i_pre + b_i).astype(f32)

    def chunks(a):
        return a.reshape(B, n_chunk, CHUNK, N_HEADS_B, -1).transpose(1, 0, 3, 2, 4)

    causal = jnp.tril(jnp.ones((CHUNK, CHUNK), dtype=bool))

    def step(carry, inp):
        C, n, m = carry
        qc, kc, vc, lf, li = inp
        b = jnp.cumsum(lf, axis=-1)
        log_d = jnp.where(causal, b[..., :, None] - b[..., None, :] + li[..., None, :], NEG)
        m_inter = b + m[..., None]
        m_t = jnp.maximum(log_d.max(axis=-1), m_inter)
        d = jnp.exp(log_d - m_t[..., None])
        s = jnp.einsum('bhtd,bhsd->bhts', qc, kc) * d
        inter = jnp.exp(m_inter - m_t)
        num = jnp.einsum('bhts,bhse->bhte', s, vc) + inter[..., None] * jnp.einsum('bhtd,bhde->bhte', qc, C)
        den = s.sum(axis=-1) + inter * jnp.einsum('bhtd,bhd->bht', qc, n)
        h = num / jnp.maximum(jnp.abs(den), jnp.exp(-m_t))[..., None]
        b_last = b[..., -1]
        log_w = b_last[..., None] - b + li
        m_new = jnp.maximum(b_last + m, log_w.max(axis=-1))
        w = jnp.exp(log_w - m_new[..., None])
        decay = jnp.exp(b_last + m - m_new)
        C_new = decay[..., None, None] * C + jnp.einsum('bhs,bhsd,bhse->bhde', w, kc, vc)
        n_new = decay[..., None] * n + jnp.einsum('bhs,bhsd->bhd', w, kc)
        return (C_new, n_new, m_new), h

    init = (jnp.zeros((B, N_HEADS_B, DH_B, DH_B), f32),
            jnp.zeros((B, N_HEADS_B, DH_B), f32),
            jnp.zeros((B, N_HEADS_B), f32))
    _, h = lax.scan(step, init, (chunks(q), chunks(k), chunks(v),
                                 chunks(log_f[..., None])[..., 0], chunks(log_i[..., None])[..., 0]))
    h = h.transpose(1, 0, 3, 2, 4).reshape(B, T, N_HEADS_B, DH_B)
    h = jax.nn.sigmoid(o_pre.astype(f32)).reshape(B, T, N_HEADS_B, DH_B) * h
    mu = h.mean(axis=-1, keepdims=True)
    var = jnp.mean((h - mu) ** 2, axis=-1, keepdims=True)
    hn = (h - mu) * lax.rsqrt(var + EPS) * mh_gain.astype(f32).reshape(N_HEADS_B, DH_B)
    y = hn.reshape(B, T, M_INNER) + skip.astype(f32) * x_c.astype(f32)
    return y.astype(x_m.dtype)


def token_mixer(h, w_in, pe_cmp_k, pe_cmp_v, w_phi1_k, w_phi2_k, w_phi1_v, w_phi2_v,
                w_conv, b_conv, w_q_m, w_k_m, w_v_m, b_i, b_f, mh_gain, skip_m,
                w_up_a, w_up_b, w_o_mix):
    B, T, _ = h.shape
    proj = h @ w_in
    sizes = [NSA_Q] + [NSA_KV] * 6 + [NSA_GATES, M_INNER, M_INNER, N_HEADS_B, N_HEADS_B, D_MODEL, D_MODEL]
    (q, k_cmp, v_cmp, k_slc, v_slc, k_win, v_win, g_nsa,
     x_m, o_pre, i_pre, f_pre, gate_a, gate_b) = jnp.split(proj, np.cumsum(sizes)[:-1].tolist(), axis=-1)

    def kv(a):
        return a.reshape(B, T, N_KV_A, DH_A)

    y_a = nsa_mixer(q.reshape(B, T, N_KV_A, HG_A, DH_A), kv(k_cmp), kv(v_cmp), kv(k_slc), kv(v_slc),
                    kv(k_win), kv(v_win), g_nsa, pe_cmp_k, pe_cmp_v, w_phi1_k, w_phi2_k, w_phi1_v, w_phi2_v)
    y_b = mlstm_mixer(x_m, o_pre, i_pre, f_pre, w_conv, b_conv, w_q_m, w_k_m, w_v_m, b_i, b_f, mh_gain, skip_m)
    merged = jax.nn.sigmoid(gate_a) * (y_a @ w_up_a) + jax.nn.sigmoid(gate_b) * (y_b @ w_up_b)
    return merged @ w_o_mix


def memory_cross_attention(h, m, w_q, w_k, w_v, w_o):
    B, T, D = h.shape
    M = m.shape[1]
    q = (h @ w_q).reshape(B, T, N_HEADS_X, DH_X)
    k = (m @ w_k).reshape(B, M, N_HEADS_X, DH_X)
    v = (m @ w_v).reshape(B, M, N_HEADS_X, DH_X)
    s = jnp.einsum('bthd,bmhd->bhtm', q, k).astype(jnp.float32) * (DH_X ** -0.5)
    p = jax.nn.softmax(s, axis=-1).astype(v.dtype)
    o = jnp.einsum('bhtm,bmhd->bthd', p, v).reshape(B, T, D)
    return o @ w_o


def moe_ffn(h, w_router, b_router, w_gu, b_gu, w_down, b_down):
    B, T, D = h.shape
    f32 = jnp.float32
    n_tok = B * T
    hf = h.reshape(n_tok, D)
    logits = (hf @ w_router + b_router).astype(f32)
    top_val, top_idx = lax.top_k(logits, TOP_K)
    gate = jax.nn.softmax(top_val, axis=-1)
    n_asg = n_tok * TOP_K
    e_flat = top_idx.reshape(n_asg)
    tok_flat = jnp.repeat(jnp.arange(n_tok, dtype=jnp.int32), TOP_K)
    w_flat = gate.reshape(n_asg)
    order = jnp.argsort(e_flat)
    e_s, tok_s, w_s = e_flat[order], tok_flat[order], w_flat[order]
    counts = jnp.bincount(e_flat, length=N_EXPERTS)
    padded = (counts + MOE_BLOCK - 1) // MOE_BLOCK * MOE_BLOCK
    p_end = jnp.cumsum(padded)
    p_start = p_end - padded
    u_start = jnp.cumsum(counts) - counts
    dest = p_start[e_s] + jnp.arange(n_asg) - u_start[e_s]
    n_blk = -(-n_asg // MOE_BLOCK) + N_EXPERTS
    n_pad = n_blk * MOE_BLOCK
    buf_tok = jnp.full((n_pad,), n_tok, jnp.int32).at[dest].set(tok_s)
    buf_w = jnp.zeros((n_pad,), f32).at[dest].set(w_s)
    blk_exp = jnp.minimum(jnp.searchsorted(p_end, jnp.arange(n_blk) * MOE_BLOCK, side='right'), N_EXPERTS - 1)
    x_buf = jnp.concatenate([hf, jnp.zeros((1, D), hf.dtype)], axis=0)[buf_tok].reshape(n_blk, MOE_BLOCK, D)

    def expert_block(args):
        xb, e = args
        gu = xb @ w_gu[e] + b_gu[e]
        g, u = gu[:, :D_FF], gu[:, D_FF:]
        g = jnp.minimum(g, SWIGLU_LIMIT)
        u = jnp.clip(u, -SWIGLU_LIMIT, SWIGLU_LIMIT)
        return ((u + 1.0) * (g * jax.nn.sigmoid(SWIGLU_ALPHA * g))) @ w_down[e] + b_down[e]

    y_buf = lax.map(expert_block, (x_buf, blk_exp)).reshape(n_pad, D)
    y = jnp.zeros((n_tok + 1, D), y_buf.dtype).at[buf_tok].add(y_buf * buf_w[:, None].astype(y_buf.dtype))
    return y[:n_tok].reshape(B, T, D)


def setup_inputs(seed: int = 0) -> dict:
    key = jax.random.key(seed)
    keys = iter(jax.random.split(key, 40))
    L = DEPTH

    def nrm(shape, scale):
        return jax.random.normal(next(keys), shape, jnp.float32) * scale

    def gain(shape):
        return 1.0 + nrm(shape, 0.02)

    return {
        'x': nrm((BATCH, SEQ, D_MODEL), 1.0),
        'mem': nrm((BATCH, N_MEM, D_MODEL), 1.0),
        'g_mix': gain((L, D_MODEL)),
        'w_in': nrm((L, D_MODEL, IN_COLS), D_MODEL ** -0.5),
        'pe_cmp_k': nrm((L, CMP_BLOCK, DH_A), 0.1),
        'pe_cmp_v': nrm((L, CMP_BLOCK, DH_A), 0.1),
        'w_phi1_k': nrm((L, CMP_BLOCK * DH_A, PHI_HIDDEN), (CMP_BLOCK * DH_A) ** -0.5),
        'w_phi2_k': nrm((L, PHI_HIDDEN, DH_A), PHI_HIDDEN ** -0.5),
        'w_phi1_v': nrm((L, CMP_BLOCK * DH_A, PHI_HIDDEN), (CMP_BLOCK * DH_A) ** -0.5),
        'w_phi2_v': nrm((L, PHI_HIDDEN, DH_A), PHI_HIDDEN ** -0.5),
        'w_conv': nrm((L, CONV_K, M_INNER), CONV_K ** -0.5),
        'b_conv': nrm((L, M_INNER), 0.01),
        'w_q_m': nrm((L, N_HEADS_B, DH_B, DH_B), DH_B ** -0.5),
        'w_k_m': nrm((L, N_HEADS_B, DH_B, DH_B), DH_B ** -0.5),
        'w_v_m': nrm((L, N_HEADS_B, DH_B, DH_B), DH_B ** -0.5),
        'b_i': nrm((L, N_HEADS_B), 0.1),
        'b_f': jnp.linspace(3.0, 6.0, N_HEADS_B)[None, :] + nrm((L, N_HEADS_B), 0.1),
        'mh_gain': gain((L, M_INNER)),
        'skip_m': gain((L, M_INNER)),
        'w_up_a': nrm((L, NSA_Q, D_MODEL), NSA_Q ** -0.5),
        'w_up_b': nrm((L, M_INNER, D_MODEL), M_INNER ** -0.5),
        'w_o_mix': nrm((L, D_MODEL, D_MODEL), D_MODEL ** -0.5),
        'g_x': gain((L, D_MODEL)),
        'g_mem': gain((L, D_MODEL)),
        'w_q_x': nrm((L, D_MODEL, D_MODEL), D_MODEL ** -0.5),
        'w_k_x': nrm((L, D_MODEL, D_MODEL), D_MODEL ** -0.5),
        'w_v_x': nrm((L, D_MODEL, D_MODEL), D_MODEL ** -0.5),
        'w_o_x': nrm((L, D_MODEL, D_MODEL), D_MODEL ** -0.5),
        'g_ffn': gain((L, D_MODEL)),
        'w_router': nrm((L, D_MODEL, N_EXPERTS), D_MODEL ** -0.5),
        'b_router': nrm((L, N_EXPERTS), 0.01),
        'w_gu': nrm((L, N_EXPERTS, D_MODEL, 2 * D_FF), D_MODEL ** -0.5),
        'b_gu': nrm((L, N_EXPERTS, 2 * D_FF), 0.01),
        'w_down': nrm((L, N_EXPERTS, D_FF, D_MODEL), D_FF ** -0.5),
        'b_down': nrm((L, N_EXPERTS, D_MODEL), 0.01),
        'g_final': gain((D_MODEL,)),
    }


def reference(x, mem, g_mix, w_in, pe_cmp_k, pe_cmp_v, w_phi1_k, w_phi2_k, w_phi1_v, w_phi2_v,
              w_conv, b_conv, w_q_m, w_k_m, w_v_m, b_i, b_f, mh_gain, skip_m,
              w_up_a, w_up_b, w_o_mix, g_x, g_mem, w_q_x, w_k_x, w_v_x, w_o_x,
              g_ffn, w_router, b_router, w_gu, b_gu, w_down, b_down, g_final):
    for l in range(DEPTH):
        x = x + token_mixer(rmsnorm(x, g_mix[l]), w_in[l], pe_cmp_k[l], pe_cmp_v[l],
                            w_phi1_k[l], w_phi2_k[l], w_phi1_v[l], w_phi2_v[l],
                            w_conv[l], b_conv[l], w_q_m[l], w_k_m[l], w_v_m[l], b_i[l], b_f[l],
                            mh_gain[l], skip_m[l], w_up_a[l], w_up_b[l], w_o_mix[l])
        x = x + memory_cross_attention(rmsnorm(x, g_x[l]), rmsnorm(mem, g_mem[l]),
                                       w_q_x[l], w_k_x[l], w_v_x[l], w_o_x[l])
        x = x + moe_ffn(rmsnorm(x, g_ffn[l]), w_router[l], b_router[l], w_gu[l], b_gu[l],
                        w_down[l], b_down[l])
    return rmsnorm(x, g_final)
```

```python
import functools

import jax
import jax.numpy as jnp
from jax import lax
from jax.experimental import pallas as pl
from jax.experimental.pallas import tpu as pltpu

F32 = jnp.float32
BF16 = jnp.bfloat16
I32 = jnp.int32

D = 1024
EPS = 1e-5
NEG = -1e30
BIG = 1e30
LOWEST = -3e38

G = 2
HG = 4
DH = 64
CMP_STRIDE = 16
SLC = 64
NSEL = 16
WIN = 512
PHI = 128
HB = 4
DHB = 128
MI = HB * DHB
HX = 4
DHX = D // HX
NE = 32
TOPK = 4
DFF = D
LIMIT = 7.0
ALPHA = 1.702
MBLK = 256

LANE = 128
SUB = 8
VMEM_LIMIT = 56 * 1024 * 1024

_O_Q = 0
_O_KCMP = 512
_O_VCMP = 640
_O_KSLC = 768
_O_VSLC = 896
_O_KWIN = 1024
_O_VWIN = 1152
_O_GNSA = 1280
_O_XM = 1304
_O_OPRE = 1816
_O_IPRE = 2328
_O_FPRE = 2332
_O_GA = 2336
_O_GB = 3360


def _cparams(sem):
    return pltpu.CompilerParams(dimension_semantics=sem, vmem_limit_bytes=VMEM_LIMIT)


def _rms(x, g):
    ms = jnp.mean(x * x, axis=-1, keepdims=True)
    return x * lax.rsqrt(ms + EPS) * g


def _dot(a, b):
    return jnp.dot(a, b, preferred_element_type=F32)


def _dot_nt(a, b):
    return lax.dot_general(a, b, (((1,), (1,)), ((), ())), preferred_element_type=F32)


def _split(x):
    hi = x.astype(BF16)
    lo = (x - hi.astype(F32)).astype(BF16)
    return hi, lo


def _inproj_kernel(x_ref, g_ref, wkn_ref, wt_ref, wgt_ref, wgn_ref, wxo_ref, wgab_ref,
                   kcmp_ref, vcmp_ref, kslc_ref, kwin_ref, tt_ref, gt_ref, gn_ref, xo_ref, gab_ref):
    hb = _rms(x_ref[0], g_ref[...]).astype(BF16)
    kn = _dot(hb, wkn_ref[...]).astype(BF16)
    kcmp_ref[0] = kn[:, 0:128]
    vcmp_ref[0] = kn[:, 128:256]
    kslc_ref[0] = kn[:, 256:384]
    kwin_ref[0] = kn[:, 384:512]
    tt_ref[0] = _dot_nt(wt_ref[...], hb).astype(BF16)
    gt_ref[0] = _dot_nt(wgt_ref[...], hb)
    gn_ref[0] = _dot(hb, wgn_ref[...])
    xo_ref[0] = _dot(hb, wxo_ref[...])
    gab_ref[0] = _dot(hb, wgab_ref[...])


def _inproj(x, g_mix, w_in, tm):
    B, T, _ = x.shape
    w = w_in
    wkn = jnp.concatenate([w[:, _O_KCMP:_O_VCMP], w[:, _O_VCMP:_O_KSLC], w[:, _O_KSLC:_O_VSLC],
                           w[:, _O_KWIN:_O_VWIN]], axis=1).astype(BF16)
    wt = jnp.concatenate([w[:, _O_Q:_O_KCMP] * (DH ** -0.5), w[:, _O_VSLC:_O_KWIN],
                          w[:, _O_VWIN:_O_GNSA]], axis=1).T.astype(BF16)
    wg = jnp.concatenate([w[:, _O_GNSA:_O_XM], w[:, _O_IPRE:_O_GA]], axis=1)
    wgt = wg.T.astype(BF16)
    wgn = jnp.pad(wg, ((0, 0), (0, LANE - 32))).astype(BF16)
    wxo = w[:, _O_XM:_O_IPRE].astype(BF16)
    wgab = w[:, _O_GA:].astype(BF16)
    nt = T // tm
    row = lambda b, i: (b, i, 0)
    col = lambda b, i: (b, 0, i)
    full = lambda b, i: (0, 0)
    out_shape = (
        jax.ShapeDtypeStruct((B, T, 128), BF16), jax.ShapeDtypeStruct((B, T, 128), BF16),
        jax.ShapeDtypeStruct((B, T, 128), BF16), jax.ShapeDtypeStruct((B, T, 128), BF16),
        jax.ShapeDtypeStruct((B, 768, T), BF16), jax.ShapeDtypeStruct((B, 32, T), F32),
        jax.ShapeDtypeStruct((B, T, 128), F32), jax.ShapeDtypeStruct((B, T, 1024), F32),
        jax.ShapeDtypeStruct((B, T, 2048), F32),
    )
    return pl.pallas_call(
        _inproj_kernel,
        grid=(B, nt),
        in_specs=[
            pl.BlockSpec((1, tm, D), row), pl.BlockSpec((1, D), full),
            pl.BlockSpec((D, 512), full), pl.BlockSpec((768, D), full), pl.BlockSpec((32, D), full),
            pl.BlockSpec((D, 128), full), pl.BlockSpec((D, 1024), full), pl.BlockSpec((D, 2048), full),
        ],
        out_specs=(
            pl.BlockSpec((1, tm, 128), row), pl.BlockSpec((1, tm, 128), row),
            pl.BlockSpec((1, tm, 128), row), pl.BlockSpec((1, tm, 128), row),
            pl.BlockSpec((1, 768, tm), col), pl.BlockSpec((1, 32, tm), col),
            pl.BlockSpec((1, tm, 128), row), pl.BlockSpec((1, tm, 1024), row),
            pl.BlockSpec((1, tm, 2048), row),
        ),
        out_shape=out_shape,
        compiler_params=_cparams(("parallel", "parallel")),
        name="inproj",
    )(x, g_mix.reshape(1, D), wkn, wt, wgt, wgn, wxo, wgab)


def _compress_kernel(xk_ref, xv_ref, pek_ref, pev_ref, wklo_ref, wkhi_ref, wk2_ref,
                     wvlo_ref, wvhi_ref, wv2t_ref, kc_ref, vct_ref):
    n_sub = xk_ref.shape[1]

    def hidden(x, pe_ref, wlo_ref, whi_ref):
        wlo = wlo_ref[...]
        whi = whi_ref[...]
        c = _dot(pe_ref[0], wlo) + _dot(pe_ref[1], whi)
        a = _dot(x, wlo)
        b = _dot(x, whi)
        pre = a + pltpu.roll(b, n_sub - 1, 0) + c[0:1, :]
        return jax.nn.gelu(pre).astype(BF16)

    hk = hidden(xk_ref[0], pek_ref, wklo_ref, wkhi_ref)
    kc_ref[0] = _dot(hk, wk2_ref[...]).astype(BF16)
    hv = hidden(xv_ref[0], pev_ref, wvlo_ref, wvhi_ref)
    vct_ref[0] = _dot_nt(wv2t_ref[...], hv).astype(BF16)


def _compress(kcmp, vcmp, pe_k, pe_v, w1k, w2k, w1v, w2v):
    B, T, _ = kcmp.shape
    n_sub = T // CMP_STRIDE
    eye = jnp.eye(G, dtype=F32)

    def big1(w1):
        lo = w1[:CMP_STRIDE * DH].reshape(CMP_STRIDE, DH, PHI)
        hi = w1[CMP_STRIDE * DH:].reshape(CMP_STRIDE, DH, PHI)
        f = lambda a: jnp.einsum('rdp,gh->rgdhp', a, eye).reshape(CMP_STRIDE * G * DH, G * PHI).astype(BF16)
        return f(lo), f(hi)

    def big2(w2):
        return jnp.einsum('pd,gh->gphd', w2, eye).reshape(G * PHI, G * DH)

    def bigpe(pe):
        f = lambda a: jnp.broadcast_to(a[:, None, :], (CMP_STRIDE, G, DH)).reshape(1, CMP_STRIDE * G * DH)
        both = jnp.stack([f(pe[:CMP_STRIDE]), f(pe[CMP_STRIDE:])], axis=0)
        return jnp.broadcast_to(both, (2, SUB, CMP_STRIDE * G * DH)).astype(BF16)

    wklo, wkhi = big1(w1k)
    wvlo, wvhi = big1(w1v)
    wk2 = big2(w2k).astype(BF16)
    wv2t = big2(w2v).T.astype(BF16)
    kw = CMP_STRIDE * G * DH
    xk = kcmp.reshape(B, n_sub, kw)
    xv = vcmp.reshape(B, n_sub, kw)
    c2 = lambda b: (0, 0)
    c3 = lambda b: (0, 0, 0)
    bsel = lambda b: (b, 0, 0)
    return pl.pallas_call(
        _compress_kernel,
        grid=(B,),
        in_specs=[
            pl.BlockSpec((1, n_sub, kw), bsel), pl.BlockSpec((1, n_sub, kw), bsel),
            pl.BlockSpec((2, SUB, kw), c3), pl.BlockSpec((2, SUB, kw), c3),
            pl.BlockSpec((kw, G * PHI), c2), pl.BlockSpec((kw, G * PHI), c2), pl.BlockSpec((G * PHI, G * DH), c2),
            pl.BlockSpec((kw, G * PHI), c2), pl.BlockSpec((kw, G * PHI), c2), pl.BlockSpec((G * DH, G * PHI), c2),
        ],
        out_specs=(pl.BlockSpec((1, n_sub, G * DH), bsel), pl.BlockSpec((1, G * DH, n_sub), bsel)),
        out_shape=(jax.ShapeDtypeStruct((B, n_sub, G * DH), BF16),
                   jax.ShapeDtypeStruct((B, G * DH, n_sub), BF16)),
        compiler_params=_cparams(("parallel",)),
        name="compress",
    )(xk, xv, bigpe(pe_k), bigpe(pe_v), wklo, wkhi, wk2, wvlo, wvhi, wv2t)


def _topk_mask_axis0(score, k):
    n = score.shape[0]
    idx = lax.broadcasted_iota(I32, score.shape, 0)
    work = score
    sel = jnp.zeros(score.shape, F32)
    for _ in range(k):
        m = jnp.max(work, axis=0, keepdims=True)
        first = jnp.min(jnp.where(work == m, idx, n), axis=0, keepdims=True)
        pick = idx == first
        sel = jnp.where(pick, 1.0, sel)
        work = jnp.where(pick, LOWEST, work)
    return sel


def _softmax0(s):
    m = jnp.max(s, axis=0, keepdims=True)
    e = jnp.exp(s - m)
    return e * (1.0 / jnp.sum(e, axis=0, keepdims=True))


def _nsa_kernel(qt_ref, kslc_ref, vslct_ref, kwin_ref, vwint_ref, kc_ref, vct_ref, gt_ref,
                at_ref, e_ref, out_ref, acc_ref, *, tq, kc_len, wk):
    T = kslc_ref.shape[1]
    n_sub = T // CMP_STRIDE
    nb = T // SLC
    nsel = min(NSEL, nb)
    q0 = pl.program_id(1) * tq
    t_row = q0 + lax.broadcasted_iota(I32, (1, tq), 1)
    kcv = kc_ref[0]
    vct = vct_ref[0]
    cmp_end = lax.broadcasted_iota(I32, (n_sub, 1), 0) * CMP_STRIDE + (2 * CMP_STRIDE - 1)
    mc = cmp_end <= t_row
    zeros_half = jnp.zeros((DH, tq), BF16)
    n_chunks = (q0 + tq + kc_len - 1) // kc_len
    wstart = pl.multiple_of(jnp.maximum(q0 - WIN, 0), tq)
    kp = wstart + lax.broadcasted_iota(I32, (wk, 1), 0)
    bias_w = jnp.where(kp <= t_row, jnp.where(kp > t_row - WIN, 0.0, NEG), NEG)
    kw = kwin_ref[0, pl.ds(wstart, wk), :]
    vwt = vwint_ref[0, :, pl.ds(wstart, wk)]
    blk = lax.broadcasted_iota(I32, (nb, tq), 0)
    cur = t_row // SLC
    at = at_ref[...]
    pieces = []
    for g in range(G):
        qps = []
        for h in range(HG):
            r0 = (g * HG + h) * DH
            qh = qt_ref[0, r0:r0 + DH, :]
            qps.append(jnp.concatenate([qh, zeros_half] if g == 0 else [zeros_half, qh], axis=0))
        pg = jnp.zeros((n_sub, tq), F32)
        o_cmp = []
        for h in range(HG):
            s = jnp.where(mc, _dot(kcv, qps[h]), NEG)
            p = jnp.where(mc, _softmax0(s), 0.0)
            pg = pg + p
            o_cmp.append(_dot(vct, p.astype(BF16)))
        pg_hi, pg_lo = _split(pg)
        imp = _dot(at, pg_hi) + _dot(at, pg_lo)
        valid = blk <= cur
        forced_or_imp = jnp.where(blk == 0, BIG, jnp.where(blk == cur, BIG, jnp.where(blk == cur - 1, BIG, imp)))
        score = jnp.where(valid, forced_or_imp, NEG)
        selb = _topk_mask_axis0(score, nsel).astype(BF16)
        acc_ref[...] = jnp.zeros(acc_ref.shape, F32)

        def body(c, carry, qps=qps, selb=selb):
            ms, ls = carry
            k0 = pl.multiple_of(c * kc_len, kc_len)
            k_c = kslc_ref[0, pl.ds(k0, kc_len), :]
            vt_c = vslct_ref[0, :, pl.ds(k0, kc_len)]
            mb = _dot(e_ref[pl.ds(k0, kc_len), :], selb)
            kpos = k0 + lax.broadcasted_iota(I32, (kc_len, 1), 0)
            bias = jnp.where(kpos <= t_row, jnp.where(mb > 0.5, 0.0, NEG), NEG)
            new_m, new_l = [], []
            for h in range(HG):
                s = _dot(k_c, qps[h]) + bias
                m_new = jnp.maximum(ms[h], jnp.max(s, axis=0, keepdims=True))
                alpha = jnp.exp(ms[h] - m_new)
                p = jnp.exp(s - m_new)
                new_l.append(alpha * ls[h] + jnp.sum(p, axis=0, keepdims=True))
                acc_ref[h] = alpha * acc_ref[h] + _dot(vt_c, p.astype(BF16))
                new_m.append(m_new)
            return tuple(new_m), tuple(new_l)

        init = (tuple(jnp.full((1, tq), NEG, F32) for _ in range(HG)),
                tuple(jnp.zeros((1, tq), F32) for _ in range(HG)))
        _, ls = lax.fori_loop(0, n_chunks, body, init)
        for h in range(HG):
            o_slc = acc_ref[h] * (1.0 / ls[h])
            p = _softmax0(_dot(kw, qps[h]) + bias_w)
            o_win = _dot(vwt, p.astype(BF16))
            gi = (g * HG + h) * 3
            sg = jax.nn.sigmoid(gt_ref[0, gi:gi + 3, :])
            o = sg[0:1] * o_cmp[h] + sg[1:2] * o_slc + sg[2:3] * o_win
            pieces.append(o[g * DH:(g + 1) * DH, :])
    yt = jnp.concatenate(pieces, axis=0)
    out_ref[0] = yt.T.astype(BF16)


def _nsa(tt, kslc, kwin, kc, vct, gt, tq, kc_len):
    B, T, _ = kslc.shape
    n_sub = T // CMP_STRIDE
    nb = T // SLC
    wk = min(WIN + tq, T)
    cidx = jnp.arange(n_sub)[None, :]
    bidx = jnp.arange(nb)[:, None]
    off = cidx - (SLC // CMP_STRIDE) * bidx
    at = (jnp.where((off >= -1) & (off <= 3), 1.0, 0.0)
          + jnp.where((off >= 0) & (off <= 2), 1.0, 0.0))
    at = jnp.where(cidx < n_sub - 1, at, 0.0).astype(BF16)
    e = (jnp.arange(T)[:, None] // SLC == jnp.arange(nb)[None, :]).astype(BF16)
    nt = T // tq
    kern = functools.partial(_nsa_kernel, tq=tq, kc_len=kc_len, wk=wk)
    bfull = lambda b, i: (b, 0, 0)
    return pl.pallas_call(
        kern,
        grid=(B, nt),
        in_specs=[
            pl.BlockSpec((1, 512, tq), lambda b, i: (b, 0, i)),
            pl.BlockSpec((1, T, 128), bfull),
            pl.BlockSpec((1, 128, T), lambda b, i: (b, 4, 0)),
            pl.BlockSpec((1, T, 128), bfull),
            pl.BlockSpec((1, 128, T), lambda b, i: (b, 5, 0)),
            pl.BlockSpec((1, n_sub, 128), bfull),
            pl.BlockSpec((1, 128, n_sub), bfull),
            pl.BlockSpec((1, 32, tq), lambda b, i: (b, 0, i)),
            pl.BlockSpec((nb, n_sub), lambda b, i: (0, 0)),
            pl.BlockSpec((T, nb), lambda b, i: (0, 0)),
        ],
        out_specs=pl.BlockSpec((1, tq, 512), lambda b, i: (b, i, 0)),
        out_shape=jax.ShapeDtypeStruct((B, T, 512), BF16),
        scratch_shapes=[pltpu.VMEM((HG, 128, tq), F32)],
        compiler_params=_cparams(("parallel", "parallel")),
        name="nsa",
    )(tt, kslc, tt, kwin, tt, kc, vct, gt, at, e)


def _log_sigmoid(x):
    return jnp.minimum(x, 0.0) - jnp.log(1.0 + jnp.exp(-jnp.abs(x)))


def _mlstm_kernel(xo_ref, gn_ref, gt_ref, bn_ref, bt_ref, wconv_ref, bconv_ref, wq_ref, wk_ref, wv_ref,
                  gain_ref, skip_ref, tri_ref, trit_ref, out_ref, xbuf, caug, mst, *, L):
    c = pl.program_id(1)

    @pl.when(c == 0)
    def _():
        xbuf[...] = jnp.zeros(xbuf.shape, F32)
        caug[...] = jnp.zeros(caug.shape, F32)
        mst[...] = jnp.zeros(mst.shape, F32)

    xbuf[0:SUB, :] = xbuf[L:L + SUB, :]
    xbuf[SUB:SUB + L, :] = xo_ref[0, :, 0:MI]
    conv = bconv_ref[...] + wconv_ref[3:4, :] * xbuf[SUB:SUB + L, :]
    for r in range(3):
        s = 3 - r
        conv = conv + wconv_ref[r:r + 1, :] * xbuf[SUB - s:SUB - s + L, :]
    xc = conv * jax.nn.sigmoid(conv)

    gn = gn_ref[0] + bn_ref[...]
    gt = gt_ref[0] + bt_ref[...]
    lf_hi, lf_lo = _split(_log_sigmoid(gn))
    tri = tri_ref[...]
    b_col_all = _dot(tri, lf_hi) + _dot(tri, lf_lo)
    lft_hi, lft_lo = _split(_log_sigmoid(gt))
    trit = trit_ref[...]
    b_row_all = _dot(lft_hi, trit) + _dot(lft_lo, trit)
    ti = lax.broadcasted_iota(I32, (L, L), 0)
    si = lax.broadcasted_iota(I32, (L, L), 1)
    causal = si <= ti
    one_col = jnp.where(lax.broadcasted_iota(I32, (L, DHB), 1) == 0, 1.0, 0.0).astype(BF16)

    outs = []
    for h in range(HB):
        lo, hi = h * DHB, (h + 1) * DHB
        xch = xc[:, lo:hi]
        xcb = xch.astype(BF16)
        q = _dot(xcb, wq_ref[h]).astype(BF16)
        k = _dot(xcb, wk_ref[h]) * (DHB ** -0.5)
        v = _dot(xo_ref[0, :, lo:hi].astype(BF16), wv_ref[h]).astype(BF16)
        v_aug = jnp.concatenate([v, one_col], axis=1)
        bc = b_col_all[:, 28 + h:29 + h]
        lic = gn[:, 24 + h:25 + h]
        br = b_row_all[28 + h:29 + h, :]
        lir = gt[24 + h:25 + h, :]
        m_prev = mst[h, 0:1, 0:1]
        log_d = jnp.where(causal, bc - br + lir, NEG)
        m_inter = bc + m_prev
        m_t = jnp.maximum(jnp.max(log_d, axis=1, keepdims=True), m_inter)
        dmat = jnp.exp(log_d - m_t)
        s = (_dot_nt(q, k.astype(BF16)) * dmat).astype(BF16)
        inter = jnp.exp(m_inter - m_t)
        c_old = caug[h]
        r = _dot(s, v_aug) + inter * _dot(q, c_old.astype(BF16))
        num = r[:, 0:DHB]
        den = r[:, DHB:DHB + 1]
        hval = num / jnp.maximum(jnp.abs(den), jnp.exp(-m_t))
        b_last = bc[L - 1:L, :]
        log_w = b_last - bc + lic
        m_new = jnp.maximum(b_last + m_prev, jnp.max(log_w, axis=0, keepdims=True))
        w = jnp.exp(log_w - m_new)
        decay = jnp.exp(b_last + m_prev - m_new)
        kwt = (k * w).T.astype(BF16)
        caug[h] = decay * c_old + _dot(kwt, v_aug)
        mst[h] = jnp.broadcast_to(m_new, (SUB, LANE))
        o = jax.nn.sigmoid(xo_ref[0, :, MI + lo:MI + hi]) * hval
        mu = jnp.mean(o, axis=-1, keepdims=True)
        var = jnp.mean((o - mu) ** 2, axis=-1, keepdims=True)
        hn = (o - mu) * lax.rsqrt(var + EPS) * gain_ref[:, lo:hi]
        outs.append(hn + skip_ref[:, lo:hi] * xch)
    out_ref[0] = jnp.concatenate(outs, axis=1).astype(BF16)


def _mlstm(xo, gn, gt, w_conv, b_conv, w_q, w_k, w_v, b_i, b_f, mh_gain, skip, L):
    B, T, _ = xo.shape
    bias = jnp.concatenate([jnp.zeros((24,), F32), b_i, b_f])
    bn = jnp.pad(bias, (0, LANE - 32)).reshape(1, LANE)
    bt = bias.reshape(32, 1)
    tri = (jnp.arange(L)[None, :] <= jnp.arange(L)[:, None]).astype(BF16)
    c2 = lambda b, c: (0, 0)
    c3 = lambda b, c: (0, 0, 0)
    return pl.pallas_call(
        functools.partial(_mlstm_kernel, L=L),
        grid=(B, T // L),
        in_specs=[
            pl.BlockSpec((1, L, 2 * MI), lambda b, c: (b, c, 0)),
            pl.BlockSpec((1, L, LANE), lambda b, c: (b, c, 0)),
            pl.BlockSpec((1, 32, L), lambda b, c: (b, 0, c)),
            pl.BlockSpec((1, LANE), c2), pl.BlockSpec((32, 1), c2),
            pl.BlockSpec((4, MI), c2), pl.BlockSpec((1, MI), c2),
            pl.BlockSpec((HB, DHB, DHB), c3), pl.BlockSpec((HB, DHB, DHB), c3), pl.BlockSpec((HB, DHB, DHB), c3),
            pl.BlockSpec((1, MI), c2), pl.BlockSpec((1, MI), c2),
            pl.BlockSpec((L, L), c2), pl.BlockSpec((L, L), c2),
        ],
        out_specs=pl.BlockSpec((1, L, MI), lambda b, c: (b, c, 0)),
        out_shape=jax.ShapeDtypeStruct((B, T, MI), BF16),
        scratch_shapes=[pltpu.VMEM((L + 2 * SUB, MI), F32), pltpu.VMEM((HB, DHB, 2 * DHB), F32),
                        pltpu.VMEM((HB, SUB, LANE), F32)],
        compiler_params=_cparams(("parallel", "arbitrary")),
        name="mlstm",
    )(xo, gn, gt, bn, bt, w_conv, b_conv.reshape(1, MI), w_q.astype(BF16), w_k.astype(BF16),
      w_v.astype(BF16), mh_gain.reshape(1, MI), skip.reshape(1, MI), tri, tri.T)


def _memkv_kernel(m_ref, g_ref, wk_ref, wv_ref, k_ref, v_ref):
    hb = _rms(m_ref[0], g_ref[...]).astype(BF16)
    k_ref[0] = _dot(hb, wk_ref[...]).astype(BF16)
    v_ref[0] = _dot(hb, wv_ref[...]).astype(BF16)


def _memkv(mem, g_mem, w_k, w_v):
    B, M, _ = mem.shape
    c2 = lambda b: (0, 0)
    bs = lambda b: (b, 0, 0)
    return pl.pallas_call(
        _memkv_kernel,
        grid=(B,),
        in_specs=[pl.BlockSpec((1, M, D), bs), pl.BlockSpec((1, D), c2),
                  pl.BlockSpec((D, D), c2), pl.BlockSpec((D, D), c2)],
        out_specs=(pl.BlockSpec((1, M, D), bs), pl.BlockSpec((1, M, D), bs)),
        out_shape=(jax.ShapeDtypeStruct((B, M, D), BF16), jax.ShapeDtypeStruct((B, M, D), BF16)),
        compiler_params=_cparams(("parallel",)),
        name="memkv",
    )(mem, g_mem.reshape(1, D), w_k.astype(BF16), w_v.astype(BF16))


def _mid_kernel(x_ref, ya_ref, yb_ref, gab_ref, wua_ref, wub_ref, wo_ref, gx_ref, wq_ref, kx_ref, vx_ref,
                wox_ref, gf_ref, wrh_ref, wrl_ref, br_ref, ltri_ref,
                x2_ref, h3_ref, ti_ref, tw_ref, cnt_ref, run_ref, *, tm):
    @pl.when((pl.program_id(0) == 0) & (pl.program_id(1) == 0))
    def _():
        run_ref[...] = jnp.zeros(run_ref.shape, F32)

    a = _dot(ya_ref[0], wua_ref[...])
    b = _dot(yb_ref[0], wub_ref[...])
    merged = jax.nn.sigmoid(gab_ref[0, :, 0:D]) * a + jax.nn.sigmoid(gab_ref[0, :, D:2 * D]) * b
    x1 = x_ref[0] + _dot(merged.astype(BF16), wo_ref[...])
    q = (_dot(_rms(x1, gx_ref[...]).astype(BF16), wq_ref[...]) * (DHX ** -0.5)).astype(BF16)
    kx = kx_ref[0]
    vx = vx_ref[0]
    heads = []
    for h in range(HX):
        lo, hi = h * DHX, (h + 1) * DHX
        s = _dot_nt(q[:, lo:hi], kx[:, lo:hi])
        m = jnp.max(s, axis=-1, keepdims=True)
        e = jnp.exp(s - m)
        p = e * (1.0 / jnp.sum(e, axis=-1, keepdims=True))
        heads.append(_dot(p.astype(BF16), vx[:, lo:hi]).astype(BF16))
    x2 = x1 + _dot(jnp.concatenate(heads, axis=1), wox_ref[...])
    x2_ref[0] = x2
    h3 = _rms(x2, gf_ref[...])
    for s in range(SUB):
        h3_ref[:, s, :] = h3[:, s * LANE:(s + 1) * LANE]
    h_hi, h_lo = _split(h3)
    logits = (_dot(h_hi, wrh_ref[...]) + _dot(h_lo, wrh_ref[...]) + _dot(h_hi, wrl_ref[...])) + br_ref[...]
    lane = lax.broadcasted_iota(I32, (tm, LANE), 1)
    work = logits
    vals, idxs = [], []
    multi = jnp.zeros((tm, LANE), F32)
    for _ in range(TOPK):
        m = jnp.max(work, axis=-1, keepdims=True)
        first = jnp.min(jnp.where(work == m, lane, LANE), axis=-1, keepdims=True)
        pick = lane == first
        vals.append(m)
        idxs.append(first)
        multi = jnp.where(pick, 1.0, multi)
        work = jnp.where(pick, LOWEST, work)
    es = [jnp.exp(v - vals[0]) for v in vals]
    inv = 1.0 / (es[0] + es[1] + es[2] + es[3])
    tw = jnp.zeros((tm, LANE), F32)
    for k in range(TOPK):
        tw = jnp.where(lane == k, es[k] * inv, tw)
    tw_ref[0] = tw
    prior = _dot(ltri_ref[...], multi.astype(BF16)) + run_ref[0:1, :]
    ti = jnp.zeros((tm, LANE), I32)
    for k in range(TOPK):
        rk = jnp.sum(jnp.where(lane == idxs[k], prior, 0.0), axis=-1, keepdims=True)
        ti = jnp.where(lane == k, idxs[k], ti)
        ti = jnp.where(lane == TOPK + k, rk.astype(I32), ti)
    ti_ref[0] = ti
    run = run_ref[...] + jnp.sum(multi, axis=0, keepdims=True)
    run_ref[...] = run
    cnt_ref[...] = run


def _mid(x, ya, yb, gab, w_up_a, w_up_b, w_o_mix, g_x, w_q_x, kx, vx, w_o_x, g_ffn, w_router, b_router, tm):
    B, T, _ = x.shape
    M = kx.shape[1]
    wr = jnp.pad(w_router, ((0, 0), (0, LANE - NE)))
    wrh, wrl = _split(wr)
    br = jnp.concatenate([b_router, jnp.full((LANE - NE,), NEG, F32)]).reshape(1, LANE)
    ltri = (jnp.arange(tm)[None, :] < jnp.arange(tm)[:, None]).astype(BF16)
    c2 = lambda b, i: (0, 0)
    row = lambda b, i: (b, i, 0)
    bs = lambda b, i: (b, 0, 0)
    nt = T // tm
    return pl.pallas_call(
        functools.partial(_mid_kernel, tm=tm),
        grid=(B, nt),
        in_specs=[
            pl.BlockSpec((1, tm, D), row), pl.BlockSpec((1, tm, 512), row), pl.BlockSpec((1, tm, MI), row),
            pl.BlockSpec((1, tm, 2 * D), row),
            pl.BlockSpec((512, D), c2), pl.BlockSpec((MI, D), c2), pl.BlockSpec((D, D), c2),
            pl.BlockSpec((1, D), c2), pl.BlockSpec((D, D), c2),
            pl.BlockSpec((1, M, D), bs), pl.BlockSpec((1, M, D), bs),
            pl.BlockSpec((D, D), c2), pl.BlockSpec((1, D), c2),
            pl.BlockSpec((D, LANE), c2), pl.BlockSpec((D, LANE), c2), pl.BlockSpec((1, LANE), c2),
            pl.BlockSpec((tm, tm), c2),
        ],
        out_specs=(
            pl.BlockSpec((1, tm, D), row),
            pl.BlockSpec((tm, SUB, LANE), lambda b, i: (b * nt + i, 0, 0)),
            pl.BlockSpec((1, tm, LANE), row), pl.BlockSpec((1, tm, LANE), row),
            pl.BlockSpec((SUB, LANE), c2),
        ),
        out_shape=(
            jax.ShapeDtypeStruct((B, T, D), F32), jax.ShapeDtypeStruct((B * T, SUB, LANE), F32),
            jax.ShapeDtypeStruct((B, T, LANE), I32), jax.ShapeDtypeStruct((B, T, LANE), F32),
            jax.ShapeDtypeStruct((SUB, LANE), F32),
        ),
        scratch_shapes=[pltpu.VMEM((SUB, LANE), F32)],
        compiler_params=_cparams(("arbitrary", "arbitrary")),
        name="mid",
    )(x, ya, yb, gab, w_up_a.astype(BF16), w_up_b.astype(BF16), w_o_mix.astype(BF16), g_x.reshape(1, D),
      w_q_x.astype(BF16), kx, vx, w_o_x.astype(BF16), g_ffn.reshape(1, D), wrh, wrl, br, ltri)


def _expert_kernel(be_ref, bt_ref, na_ref, h3_hbm, wgu_ref, bgu_ref, wdn_ref, bdn_ref, out_ref, xbuf, sem):
    i = pl.program_id(0)
    nact = na_ref[0]

    def issue(blk, slot):
        base = blk * MBLK

        def body(r, carry):
            tok = bt_ref[base + r]
            pltpu.make_async_copy(h3_hbm.at[tok], xbuf.at[slot, r], sem.at[slot]).start()
            return carry

        lax.fori_loop(0, MBLK, body, 0)

    @pl.when(i == 0)
    def _():
        issue(0, 0)

    slot = i % 2

    @pl.when(i < nact)
    def _():
        pltpu.make_async_copy(h3_hbm.at[pl.ds(0, MBLK)], xbuf.at[slot], sem.at[slot]).wait()

        @pl.when(i + 1 < nact)
        def _():
            issue(i + 1, 1 - slot)

        x = jnp.concatenate([xbuf[slot, :, s, :] for s in range(SUB)], axis=1).astype(BF16)
        gu = _dot(x, wgu_ref[0]) + bgu_ref[0]
        g = jnp.minimum(gu[:, 0:DFF], LIMIT)
        u = jnp.clip(gu[:, DFF:2 * DFF], -LIMIT, LIMIT)
        act = (u + 1.0) * (g * jax.nn.sigmoid(ALPHA * g))
        y = _dot(act.astype(BF16), wdn_ref[0]) + bdn_ref[0]
        for s in range(SUB):
            out_ref[:, s, :] = y[:, s * LANE:(s + 1) * LANE]

    @pl.when(i >= nact)
    def _():
        out_ref[...] = jnp.zeros(out_ref.shape, F32)


def _experts(h3, blk_exp, buf_tok, nact, w_gu, b_gu, w_down, b_down):
    n_blk = blk_exp.shape[0]
    wsel = lambda i, be, bt, na: (be[i], 0, 0)
    grid_spec = pltpu.PrefetchScalarGridSpec(
        num_scalar_prefetch=3,
        grid=(n_blk,),
        in_specs=[
            pl.BlockSpec(memory_space=pl.ANY),
            pl.BlockSpec((1, D, 2 * DFF), wsel), pl.BlockSpec((1, 1, 2 * DFF), wsel),
            pl.BlockSpec((1, DFF, D), wsel), pl.BlockSpec((1, 1, D), wsel),
        ],
        out_specs=pl.BlockSpec((MBLK, SUB, LANE), lambda i, be, bt, na: (i, 0, 0)),
        scratch_shapes=[pltpu.VMEM((2, MBLK, SUB, LANE), F32), pltpu.SemaphoreType.DMA((2,))],
    )
    return pl.pallas_call(
        _expert_kernel,
        grid_spec=grid_spec,
        out_shape=jax.ShapeDtypeStruct((n_blk * MBLK, SUB, LANE), F32),
        compiler_params=_cparams(("arbitrary",)),
        name="experts",
    )(blk_exp, buf_tok, nact, h3, w_gu.astype(BF16), b_gu.reshape(NE, 1, 2 * DFF),
      w_down.astype(BF16), b_down.reshape(NE, 1, D))


def _combine_kernel(pos_ref, x2_ref, tw_ref, gfin_ref, y_hbm, out_ref, ybuf, sem, *, tc, final_norm):
    i = pl.program_id(0)
    nt = pl.num_programs(0)

    def issue(tile, slot):
        base = tile * tc * TOPK

        for k in range(TOPK):
            def body(r, carry, k=k):
                p = pos_ref[base + r * TOPK + k]
                pltpu.make_async_copy(y_hbm.at[p], ybuf.at[slot, k * tc + r], sem.at[slot]).start()
                return carry

            lax.fori_loop(0, tc, body, 0)

    @pl.when(i == 0)
    def _():
        issue(0, 0)

    slot = i % 2
    pltpu.make_async_copy(y_hbm.at[pl.ds(0, TOPK * tc)], ybuf.at[slot], sem.at[slot]).wait()

    @pl.when(i + 1 < nt)
    def _():
        issue(i + 1, 1 - slot)

    ws = [tw_ref[:, k:k + 1] for k in range(TOPK)]
    accs = []
    ssq = jnp.zeros((tc, LANE), F32)
    for s in range(SUB):
        a = x2_ref[:, s * LANE:(s + 1) * LANE]
        for k in range(TOPK):
            a = a + ws[k] * ybuf[slot, k * tc:(k + 1) * tc, s, :]
        accs.append(a)
        ssq = ssq + a * a
    if not final_norm:
        for s in range(SUB):
            out_ref[:, s * LANE:(s + 1) * LANE] = accs[s]
        return
    scale = lax.rsqrt(jnp.sum(ssq, axis=-1, keepdims=True) * (1.0 / D) + EPS)
    for s in range(SUB):
        out_ref[:, s * LANE:(s + 1) * LANE] = accs[s] * scale * gfin_ref[:, s * LANE:(s + 1) * LANE]


def _combine(pos_flat, x2, tw, g_final, ybuf, tc, final_norm):
    N = x2.shape[0]
    grid_spec = pltpu.PrefetchScalarGridSpec(
        num_scalar_prefetch=1,
        grid=(N // tc,),
        in_specs=[
            pl.BlockSpec((tc, D), lambda i, p: (i, 0)),
            pl.BlockSpec((tc, LANE), lambda i, p: (i, 0)),
            pl.BlockSpec((1, D), lambda i, p: (0, 0)),
            pl.BlockSpec(memory_space=pl.ANY),
        ],
        out_specs=pl.BlockSpec((tc, D), lambda i, p: (i, 0)),
        scratch_shapes=[pltpu.VMEM((2, TOPK * tc, SUB, LANE), F32), pltpu.SemaphoreType.DMA((2,))],
    )
    return pl.pallas_call(
        functools.partial(_combine_kernel, tc=tc, final_norm=final_norm),
        grid_spec=grid_spec,
        out_shape=jax.ShapeDtypeStruct((N, D), F32),
        compiler_params=_cparams(("arbitrary",)),
        name="combine",
    )(pos_flat, x2, tw, g_final.reshape(1, D), ybuf)


def _moe_slots(ti, counts, n_tok):
    top_idx = ti[:, 0:TOPK]
    rank = ti[:, TOPK:2 * TOPK]
    padded = (counts + MBLK - 1) // MBLK * MBLK
    p_end = jnp.cumsum(padded)
    p_start = p_end - padded
    pos = p_start[top_idx] + rank
    n_blk = n_tok * TOPK // MBLK + NE
    blk_exp = jnp.minimum(jnp.searchsorted(p_end, jnp.arange(n_blk, dtype=I32) * MBLK, side='right'),
                          NE - 1).astype(I32)
    nact = (p_end[-1] // MBLK).astype(I32).reshape(1)
    tok = jnp.broadcast_to(jnp.arange(n_tok, dtype=I32)[:, None], (n_tok, TOPK))
    buf_tok = jnp.zeros((n_blk * MBLK,), I32).at[pos.reshape(-1)].set(tok.reshape(-1))
    return pos.reshape(-1).astype(I32), blk_exp, buf_tok, nact


def _layer(x, mem, g_mix, w_in, pe_cmp_k, pe_cmp_v, w_phi1_k, w_phi2_k, w_phi1_v, w_phi2_v,
           w_conv, b_conv, w_q_m, w_k_m, w_v_m, b_i, b_f, mh_gain, skip_m,
           w_up_a, w_up_b, w_o_mix, g_x, g_mem, w_q_x, w_k_x, w_v_x, w_o_x,
           g_ffn, w_router, b_router, w_gu, b_gu, w_down, b_down, g_final, final_norm):
    B, T, _ = x.shape
    kcmp, vcmp, kslc, kwin, tt, gt, gn, xo, gab = _inproj(x, g_mix, w_in, tm=256)
    kc, vct = _compress(kcmp, vcmp, pe_cmp_k, pe_cmp_v, w_phi1_k, w_phi2_k, w_phi1_v, w_phi2_v)
    ya = _nsa(tt, kslc, kwin, kc, vct, gt, tq=128, kc_len=512)
    yb = _mlstm(xo, gn, gt, w_conv, b_conv, w_q_m, w_k_m, w_v_m, b_i, b_f, mh_gain, skip_m, L=256)
    kx, vx = _memkv(mem, g_mem, w_k_x, w_v_x)
    x2, h3, ti, tw, cnt = _mid(x, ya, yb, gab, w_up_a, w_up_b, w_o_mix, g_x, w_q_x, kx, vx, w_o_x,
                               g_ffn, w_router, b_router, tm=256)
    n_tok = B * T
    counts = cnt[0, 0:NE].astype(I32)
    pos, blk_exp, buf_tok, nact = _moe_slots(ti.reshape(n_tok, LANE), counts, n_tok)
    ybuf = _experts(h3, blk_exp, buf_tok, nact, w_gu, b_gu, w_down, b_down)
    out = _combine(pos, x2.reshape(n_tok, D), tw.reshape(n_tok, LANE), g_final, ybuf, tc=128,
                   final_norm=final_norm)
    return out.reshape(B, T, D)


def kernel(x, mem, g_mix, w_in, pe_cmp_k, pe_cmp_v, w_phi1_k, w_phi2_k, w_phi1_v, w_phi2_v, w_conv, b_conv, w_q_m, w_k_m, w_v_m, b_i, b_f, mh_gain, skip_m, w_up_a, w_up_b, w_o_mix, g_x, g_mem, w_q_x, w_k_x, w_v_x, w_o_x, g_ffn, w_router, b_router, w_gu, b_gu, w_down, b_down, g_final):
    layers = (g_mix, w_in, pe_cmp_k, pe_cmp_v, w_phi1_k, w_phi2_k, w_phi1_v, w_phi2_v, w_conv, b_conv,
              w_q_m, w_k_m, w_v_m, b_i, b_f, mh_gain, skip_m, w_up_a, w_up_b, w_o_mix, g_x, g_mem,
              w_q_x, w_k_x, w_v_x, w_o_x, g_ffn, w_router, b_router, w_gu, b_gu, w_down, b_down)
    depth = g_mix.shape[0]
    for l in range(depth):
        x = _layer(x, mem, *(w[l] for w in layers), g_final, final_norm=(l == depth - 1))
    return x
```

```python
import functools

import jax
import jax.numpy as jnp
from jax import lax
from jax.experimental import pallas as pl
from jax.experimental.pallas import tpu as pltpu

F32 = jnp.float32
BF16 = jnp.bfloat16
I32 = jnp.int32

D = 1024
EPS = 1e-5
NEG = -1e30
BIG = 1e30
LOWEST = -3e38

G = 2
HG = 4
DH = 64
CMP_STRIDE = 16
SLC = 64
NSEL = 16
WIN = 512
PHI = 128
NBP = 128
LOG2E = 1.4426950408889634
HB = 4
DHB = 128
MI = HB * DHB
HX = 4
DHX = D // HX
NE = 32
TOPK = 4
DFF = D
LIMIT = 7.0
ALPHA = 1.702
MBLK = 256

LANE = 128
SUB = 8
VMEM_LIMIT = 56 * 1024 * 1024

_O_Q = 0
_O_KCMP = 512
_O_VCMP = 640
_O_KSLC = 768
_O_VSLC = 896
_O_KWIN = 1024
_O_VWIN = 1152
_O_GNSA = 1280
_O_XM = 1304
_O_OPRE = 1816
_O_IPRE = 2328
_O_FPRE = 2332
_O_GA = 2336
_O_GB = 3360


def _cparams(sem):
    return pltpu.CompilerParams(dimension_semantics=sem, vmem_limit_bytes=VMEM_LIMIT)


def _rms(x, g):
    ms = jnp.mean(x * x, axis=-1, keepdims=True)
    return x * lax.rsqrt(ms + EPS) * g


def _dot(a, b):
    return jnp.dot(a, b, preferred_element_type=F32)


def _dot_nt(a, b):
    return lax.dot_general(a, b, (((1,), (1,)), ((), ())), preferred_element_type=F32)


def _split(x):
    hi = x.astype(BF16)
    lo = (x - hi.astype(F32)).astype(BF16)
    return hi, lo


def _inproj_kernel(x_ref, g_ref, wkn_ref, wt_ref, wgt_ref, wgn_ref, wxo_ref, wgab_ref,
                   kcmp_ref, vcmp_ref, kslc_ref, kwin_ref, tt_ref, gt_ref, gn_ref, xo_ref, gab_ref):
    hb = _rms(x_ref[0], g_ref[...]).astype(BF16)
    kn = _dot(hb, wkn_ref[...]).astype(BF16)
    kcmp_ref[0] = kn[:, 0:128]
    vcmp_ref[0] = kn[:, 128:256]
    kslc_ref[0] = kn[:, 256:384]
    kwin_ref[0] = kn[:, 384:512]
    tt_ref[0] = _dot_nt(wt_ref[...], hb).astype(BF16)
    gt_ref[0] = _dot_nt(wgt_ref[...], hb)
    gn_ref[0] = _dot(hb, wgn_ref[...])
    xo_ref[0] = _dot(hb, wxo_ref[...])
    gab_ref[0] = _dot(hb, wgab_ref[...])


def _inproj(x, g_mix, w_in, tm):
    B, T, _ = x.shape
    w = w_in
    wkn = jnp.concatenate([w[:, _O_KCMP:_O_VCMP], w[:, _O_VCMP:_O_KSLC], w[:, _O_KSLC:_O_VSLC],
                           w[:, _O_KWIN:_O_VWIN]], axis=1).astype(BF16)
    wt = jnp.concatenate([w[:, _O_Q:_O_KCMP] * (DH ** -0.5 * LOG2E), w[:, _O_VSLC:_O_KWIN],
                          w[:, _O_VWIN:_O_GNSA]], axis=1).T.astype(BF16)
    wg = jnp.concatenate([w[:, _O_GNSA:_O_XM], w[:, _O_IPRE:_O_GA]], axis=1)
    wgt = wg.T.astype(BF16)
    wgn = jnp.pad(wg, ((0, 0), (0, LANE - 32))).astype(BF16)
    wxo = w[:, _O_XM:_O_IPRE].astype(BF16)
    wgab = w[:, _O_GA:].astype(BF16)
    nt = T // tm
    row = lambda b, i: (b, i, 0)
    col = lambda b, i: (b, 0, i)
    full = lambda b, i: (0, 0)
    out_shape = (
        jax.ShapeDtypeStruct((B, T, 128), BF16), jax.ShapeDtypeStruct((B, T, 128), BF16),
        jax.ShapeDtypeStruct((B, T, 128), BF16), jax.ShapeDtypeStruct((B, T, 128), BF16),
        jax.ShapeDtypeStruct((B, 768, T), BF16), jax.ShapeDtypeStruct((B, 32, T), F32),
        jax.ShapeDtypeStruct((B, T, 128), F32), jax.ShapeDtypeStruct((B, T, 1024), F32),
        jax.ShapeDtypeStruct((B, T, 2048), F32),
    )
    return pl.pallas_call(
        _inproj_kernel,
        grid=(B, nt),
        in_specs=[
            pl.BlockSpec((1, tm, D), row), pl.BlockSpec((1, D), full),
            pl.BlockSpec((D, 512), full), pl.BlockSpec((768, D), full), pl.BlockSpec((32, D), full),
            pl.BlockSpec((D, 128), full), pl.BlockSpec((D, 1024), full), pl.BlockSpec((D, 2048), full),
        ],
        out_specs=(
            pl.BlockSpec((1, tm, 128), row), pl.BlockSpec((1, tm, 128), row),
            pl.BlockSpec((1, tm, 128), row), pl.BlockSpec((1, tm, 128), row),
            pl.BlockSpec((1, 768, tm), col), pl.BlockSpec((1, 32, tm), col),
            pl.BlockSpec((1, tm, 128), row), pl.BlockSpec((1, tm, 1024), row),
            pl.BlockSpec((1, tm, 2048), row),
        ),
        out_shape=out_shape,
        compiler_params=_cparams(("parallel", "parallel")),
        name="inproj",
    )(x, g_mix.reshape(1, D), wkn, wt, wgt, wgn, wxo, wgab)


def _compress_kernel(xk_ref, xv_ref, pek_ref, pev_ref, wklo_ref, wkhi_ref, wk2_ref,
                     wvlo_ref, wvhi_ref, wv2t_ref, kc_ref, vct_ref):
    n_sub = xk_ref.shape[1]

    def hidden(x, pe_ref, wlo_ref, whi_ref):
        wlo = wlo_ref[...]
        whi = whi_ref[...]
        c = _dot(pe_ref[0], wlo) + _dot(pe_ref[1], whi)
        a = _dot(x, wlo)
        b = _dot(x, whi)
        pre = a + pltpu.roll(b, n_sub - 1, 0) + c[0:1, :]
        return jax.nn.gelu(pre).astype(BF16)

    hk = hidden(xk_ref[0], pek_ref, wklo_ref, wkhi_ref)
    kc_ref[0] = _dot(hk, wk2_ref[...]).astype(BF16)
    hv = hidden(xv_ref[0], pev_ref, wvlo_ref, wvhi_ref)
    vct_ref[0] = _dot_nt(wv2t_ref[...], hv).astype(BF16)


def _compress(kcmp, vcmp, pe_k, pe_v, w1k, w2k, w1v, w2v):
    B, T, _ = kcmp.shape
    n_sub = T // CMP_STRIDE
    eye = jnp.eye(G, dtype=F32)

    def big1(w1):
        lo = w1[:CMP_STRIDE * DH].reshape(CMP_STRIDE, DH, PHI)
        hi = w1[CMP_STRIDE * DH:].reshape(CMP_STRIDE, DH, PHI)
        f = lambda a: jnp.einsum('rdp,gh->rgdhp', a, eye).reshape(CMP_STRIDE * G * DH, G * PHI).astype(BF16)
        return f(lo), f(hi)

    def big2(w2):
        return jnp.einsum('pd,gh->gphd', w2, eye).reshape(G * PHI, G * DH)

    def bigpe(pe):
        f = lambda a: jnp.broadcast_to(a[:, None, :], (CMP_STRIDE, G, DH)).reshape(1, CMP_STRIDE * G * DH)
        both = jnp.stack([f(pe[:CMP_STRIDE]), f(pe[CMP_STRIDE:])], axis=0)
        return jnp.broadcast_to(both, (2, SUB, CMP_STRIDE * G * DH)).astype(BF16)

    wklo, wkhi = big1(w1k)
    wvlo, wvhi = big1(w1v)
    wk2 = big2(w2k).astype(BF16)
    wv2t = big2(w2v).T.astype(BF16)
    kw = CMP_STRIDE * G * DH
    xk = kcmp.reshape(B, n_sub, kw)
    xv = vcmp.reshape(B, n_sub, kw)
    c2 = lambda b: (0, 0)
    c3 = lambda b: (0, 0, 0)
    bsel = lambda b: (b, 0, 0)
    return pl.pallas_call(
        _compress_kernel,
        grid=(B,),
        in_specs=[
            pl.BlockSpec((1, n_sub, kw), bsel), pl.BlockSpec((1, n_sub, kw), bsel),
            pl.BlockSpec((2, SUB, kw), c3), pl.BlockSpec((2, SUB, kw), c3),
            pl.BlockSpec((kw, G * PHI), c2), pl.BlockSpec((kw, G * PHI), c2), pl.BlockSpec((G * PHI, G * DH), c2),
            pl.BlockSpec((kw, G * PHI), c2), pl.BlockSpec((kw, G * PHI), c2), pl.BlockSpec((G * DH, G * PHI), c2),
        ],
        out_specs=(pl.BlockSpec((1, n_sub, G * DH), bsel), pl.BlockSpec((1, G * DH, n_sub), bsel)),
        out_shape=(jax.ShapeDtypeStruct((B, n_sub, G * DH), BF16),
                   jax.ShapeDtypeStruct((B, G * DH, n_sub), BF16)),
        compiler_params=_cparams(("parallel",)),
        name="compress",
    )(xk, xv, bigpe(pe_k), bigpe(pe_v), wklo, wkhi, wk2, wvlo, wvhi, wv2t)


def _topk_mask_axis0(score, k):
    n = score.shape[0]
    idx = lax.broadcasted_iota(I32, score.shape, 0)
    work = score
    sel = jnp.zeros(score.shape, F32)
    for _ in range(k):
        m = jnp.max(work, axis=0, keepdims=True)
        first = jnp.min(jnp.where(work == m, idx, n), axis=0, keepdims=True)
        pick = idx == first
        sel = jnp.where(pick, 1.0, sel)
        work = jnp.where(pick, LOWEST, work)
    return sel


def _nsa_kernel(qt_ref, kslc_ref, vslct_ref, kwin_ref, vwint_ref, kc_ref, vct_ref, gt_ref,
                at_ref, e_ref, out_ref, acc_ref, *, tq, wk):
    T = kslc_ref.shape[1]
    n_sub = T // CMP_STRIDE
    nsel = min(NSEL, T // SLC)
    i = pl.program_id(1)
    q0 = i * tq
    t_row = q0 + lax.broadcasted_iota(I32, (1, tq), 1)
    kcv = kc_ref[0]
    cmp_end = lax.broadcasted_iota(I32, (n_sub, 1), 0) * CMP_STRIDE + (2 * CMP_STRIDE - 1)
    bias_c = jnp.where(cmp_end <= t_row, 0.0, NEG)
    col_ok = t_row >= 2 * CMP_STRIDE - 1
    wstart = pl.multiple_of(jnp.maximum(q0 - WIN, 0), tq)
    kp = wstart + lax.broadcasted_iota(I32, (wk, 1), 0)
    bias_w = jnp.where(kp <= t_row, jnp.where(kp > t_row - WIN, 0.0, NEG), NEG)
    kw = kwin_ref[0, pl.ds(wstart, wk), :]
    tri_bias = jnp.where(lax.broadcasted_iota(I32, (tq, tq), 0) <= lax.broadcasted_iota(I32, (tq, tq), 1),
                         0.0, NEG)
    blk = lax.broadcasted_iota(I32, (NBP, tq), 0)
    cur = t_row // SLC
    at = at_ref[...]
    zeros_half = jnp.zeros((DH, tq), BF16)
    ones_k = jnp.ones((2 * SUB, tq), BF16)
    pieces = []
    for g in range(G):
        glo, ghi = g * DH, (g + 1) * DH
        qps = []
        for h in range(HG):
            r0 = (g * HG + h) * DH
            qh = qt_ref[0, r0:r0 + DH, :]
            qps.append(jnp.concatenate([qh, zeros_half] if g == 0 else [zeros_half, qh], axis=0))
        qp_all = jnp.concatenate(qps, axis=1)

        def attend(s_h, v_aug, col_scale=None):
            m = jnp.max(s_h, axis=0, keepdims=True)
            e = jnp.exp2(s_h - m)
            r = _dot(v_aug, e.astype(BF16))
            inv = 1.0 / r[DH:DH + 1]
            if col_scale is not None:
                inv = jnp.where(col_scale, inv, 0.0)
            return r[0:DH] * inv, e, inv

        vc_aug = jnp.concatenate([vct_ref[0, glo:ghi, :], jnp.ones((2 * SUB, n_sub), BF16)], axis=0)
        s_c = _dot(kcv, qp_all)
        pg = jnp.zeros((n_sub, tq), F32)
        o_cmp = []
        for h in range(HG):
            o, e, inv = attend(s_c[:, h * tq:(h + 1) * tq] + bias_c, vc_aug, col_ok)
            o_cmp.append(o)
            pg = pg + e * inv
        pg_hi, pg_lo = _split(pg)
        imp = _dot(at, pg_hi) + _dot(at, pg_lo)
        forced_or_imp = jnp.where(blk == 0, BIG, jnp.where(blk == cur, BIG, jnp.where(blk == cur - 1, BIG, imp)))
        score = jnp.where(blk <= cur, forced_or_imp, NEG)
        sel = _topk_mask_axis0(score, nsel)
        selbias = jnp.where(sel > 0.5, 0.0, NEG).astype(BF16)
        q_aug = jnp.concatenate([qp_all, jnp.concatenate([selbias] * HG, axis=1)], axis=0)
        acc_ref[...] = jnp.zeros(acc_ref.shape, F32)

        def chunk(c, ms, diag, q_aug=q_aug, glo=glo, ghi=ghi):
            k0 = pl.multiple_of(c * tq, tq)
            k_aug = jnp.concatenate([kslc_ref[0, pl.ds(k0, tq), :], e_ref[pl.ds(k0, tq), :]], axis=1)
            v_aug = jnp.concatenate([vslct_ref[0, glo:ghi, pl.ds(k0, tq)], ones_k], axis=0)
            s = _dot(k_aug, q_aug)
            new_ms = []
            for h in range(HG):
                s_h = s[:, h * tq:(h + 1) * tq]
                if diag:
                    s_h = s_h + tri_bias
                m_new = jnp.maximum(ms[h], jnp.max(s_h, axis=0, keepdims=True))
                alpha = jnp.exp2(ms[h] - m_new)
                p = jnp.exp2(s_h - m_new).astype(BF16)
                acc_ref[h] = alpha * acc_ref[h] + _dot(v_aug, p)
                new_ms.append(m_new)
            return tuple(new_ms)

        ms = tuple(jnp.full((1, tq), NEG, F32) for _ in range(HG))
        ms = lax.fori_loop(0, i, lambda c, ms: chunk(c, ms, False), ms)
        chunk(i, ms, True)
        vw_aug = jnp.concatenate([vwint_ref[0, glo:ghi, pl.ds(wstart, wk)], jnp.ones((2 * SUB, wk), BF16)], axis=0)
        s_w = _dot(kw, qp_all)
        for h in range(HG):
            accv = acc_ref[h]
            o_slc = accv[0:DH] * (1.0 / accv[DH:DH + 1])
            o_win, _, _ = attend(s_w[:, h * tq:(h + 1) * tq] + bias_w, vw_aug)
            gi = (g * HG + h) * 3
            sg = jax.nn.sigmoid(gt_ref[0, gi:gi + 3, :])
            pieces.append(sg[0:1] * o_cmp[h] + sg[1:2] * o_slc + sg[2:3] * o_win)
    yt = jnp.concatenate(pieces, axis=0)
    out_ref[0] = yt.T.astype(BF16)


def _nsa(tt, kslc, kwin, kc, vct, gt, tq):
    B, T, _ = kslc.shape
    n_sub = T // CMP_STRIDE
    nb = T // SLC
    assert nb <= NBP and T % tq == 0
    wk = min(WIN + tq, T)
    cidx = jnp.arange(n_sub)[None, :]
    bidx = jnp.arange(NBP)[:, None]
    off = cidx - (SLC // CMP_STRIDE) * bidx
    at = (jnp.where((off >= -1) & (off <= 3), 1.0, 0.0)
          + jnp.where((off >= 0) & (off <= 2), 1.0, 0.0))
    at = jnp.where((cidx < n_sub - 1) & (bidx < nb), at, 0.0).astype(BF16)
    e = (jnp.arange(T)[:, None] // SLC == jnp.arange(NBP)[None, :]).astype(BF16)
    nt = T // tq
    kern = functools.partial(_nsa_kernel, tq=tq, wk=wk)
    bfull = lambda b, i: (b, 0, 0)
    return pl.pallas_call(
        kern,
        grid=(B, nt),
        in_specs=[
            pl.BlockSpec((1, 512, tq), lambda b, i: (b, 0, i)),
            pl.BlockSpec((1, T, 128), bfull),
            pl.BlockSpec((1, 128, T), lambda b, i: (b, 4, 0)),
            pl.BlockSpec((1, T, 128), bfull),
            pl.BlockSpec((1, 128, T), lambda b, i: (b, 5, 0)),
            pl.BlockSpec((1, n_sub, 128), bfull),
            pl.BlockSpec((1, 128, n_sub), bfull),
            pl.BlockSpec((1, 32, tq), lambda b, i: (b, 0, i)),
            pl.BlockSpec((NBP, n_sub), lambda b, i: (0, 0)),
            pl.BlockSpec((T, NBP), lambda b, i: (0, 0)),
        ],
        out_specs=pl.BlockSpec((1, tq, 512), lambda b, i: (b, i, 0)),
        out_shape=jax.ShapeDtypeStruct((B, T, 512), BF16),
        scratch_shapes=[pltpu.VMEM((HG, DH + 2 * SUB, tq), F32)],
        compiler_params=_cparams(("parallel", "parallel")),
        name="nsa",
    )(tt, kslc, tt, kwin, tt, kc, vct, gt, at, e)


def _log_sigmoid(x):
    return jnp.minimum(x, 0.0) - jnp.log(1.0 + jnp.exp(-jnp.abs(x)))


def _mlstm_kernel(xo_ref, gn_ref, gt_ref, bn_ref, bt_ref, wconv_ref, bconv_ref, wq_ref, wk_ref, wv_ref,
                  gain_ref, skip_ref, tri_ref, trit_ref, out_ref, xbuf, caug, mst, *, L):
    c = pl.program_id(1)

    @pl.when(c == 0)
    def _():
        xbuf[...] = jnp.zeros(xbuf.shape, F32)
        caug[...] = jnp.zeros(caug.shape, F32)
        mst[...] = jnp.zeros(mst.shape, F32)

    xbuf[0:SUB, :] = xbuf[L:L + SUB, :]
    xbuf[SUB:SUB + L, :] = xo_ref[0, :, 0:MI]
    conv = bconv_ref[...] + wconv_ref[3:4, :] * xbuf[SUB:SUB + L, :]
    for r in range(3):
        s = 3 - r
        conv = conv + wconv_ref[r:r + 1, :] * xbuf[SUB - s:SUB - s + L, :]
    xc = conv * jax.nn.sigmoid(conv)

    gn = gn_ref[0] + bn_ref[...]
    gt = gt_ref[0] + bt_ref[...]
    lf_hi, lf_lo = _split(_log_sigmoid(gn))
    tri = tri_ref[...]
    b_col_all = _dot(tri, lf_hi) + _dot(tri, lf_lo)
    lft_hi, lft_lo = _split(_log_sigmoid(gt))
    trit = trit_ref[...]
    b_row_all = _dot(lft_hi, trit) + _dot(lft_lo, trit)
    ti = lax.broadcasted_iota(I32, (L, L), 0)
    si = lax.broadcasted_iota(I32, (L, L), 1)
    causal = si <= ti
    one_col = jnp.where(lax.broadcasted_iota(I32, (L, DHB), 1) == 0, 1.0, 0.0).astype(BF16)

    outs = []
    for h in range(HB):
        lo, hi = h * DHB, (h + 1) * DHB
        xch = xc[:, lo:hi]
        xcb = xch.astype(BF16)
        q = _dot(xcb, wq_ref[h]).astype(BF16)
        k = _dot(xcb, wk_ref[h]) * (DHB ** -0.5)
        v = _dot(xo_ref[0, :, lo:hi].astype(BF16), wv_ref[h]).astype(BF16)
        v_aug = jnp.concatenate([v, one_col], axis=1)
        bc = b_col_all[:, 28 + h:29 + h]
        lic = gn[:, 24 + h:25 + h]
        br = b_row_all[28 + h:29 + h, :]
        lir = gt[24 + h:25 + h, :]
        m_prev = mst[h, 0:1, 0:1]
        log_d = jnp.where(causal, bc - br + lir, NEG)
        m_inter = bc + m_prev
        m_t = jnp.maximum(jnp.max(log_d, axis=1, keepdims=True), m_inter)
        dmat = jnp.exp(log_d - m_t)
        s = (_dot_nt(q, k.astype(BF16)) * dmat).astype(BF16)
        inter = jnp.exp(m_inter - m_t)
        c_old = caug[h]
        r = _dot(s, v_aug) + inter * _dot(q, c_old.astype(BF16))
        num = r[:, 0:DHB]
        den = r[:, DHB:DHB + 1]
        hval = num / jnp.maximum(jnp.abs(den), jnp.exp(-m_t))
        b_last = bc[L - 1:L, :]
        log_w = b_last - bc + lic
        m_new = jnp.maximum(b_last + m_prev, jnp.max(log_w, axis=0, keepdims=True))
        w = jnp.exp(log_w - m_new)
        decay = jnp.exp(b_last + m_prev - m_new)
        kwt = (k * w).T.astype(BF16)
        caug[h] = decay * c_old + _dot(kwt, v_aug)
        mst[h] = jnp.broadcast_to(m_new, (SUB, LANE))
        o = jax.nn.sigmoid(xo_ref[0, :, MI + lo:MI + hi]) * hval
        mu = jnp.mean(o, axis=-1, keepdims=True)
        var = jnp.mean((o - mu) ** 2, axis=-1, keepdims=True)
        hn = (o - mu) * lax.rsqrt(var + EPS) * gain_ref[:, lo:hi]
        outs.append(hn + skip_ref[:, lo:hi] * xch)
    out_ref[0] = jnp.concatenate(outs, axis=1).astype(BF16)


def _mlstm(xo, gn, gt, w_conv, b_conv, w_q, w_k, w_v, b_i, b_f, mh_gain, skip, L):
    B, T, _ = xo.shape
    bias = jnp.concatenate([jnp.zeros((24,), F32), b_i, b_f])
    bn = jnp.pad(bias, (0, LANE - 32)).reshape(1, LANE)
    bt = bias.reshape(32, 1)
    tri = (jnp.arange(L)[None, :] <= jnp.arange(L)[:, None]).astype(BF16)
    c2 = lambda b, c: (0, 0)
    c3 = lambda b, c: (0, 0, 0)
    return pl.pallas_call(
        functools.partial(_mlstm_kernel, L=L),
        grid=(B, T // L),
        in_specs=[
            pl.BlockSpec((1, L, 2 * MI), lambda b, c: (b, c, 0)),
            pl.BlockSpec((1, L, LANE), lambda b, c: (b, c, 0)),
            pl.BlockSpec((1, 32, L), lambda b, c: (b, 0, c)),
            pl.BlockSpec((1, LANE), c2), pl.BlockSpec((32, 1), c2),
            pl.BlockSpec((4, MI), c2), pl.BlockSpec((1, MI), c2),
            pl.BlockSpec((HB, DHB, DHB), c3), pl.BlockSpec((HB, DHB, DHB), c3), pl.BlockSpec((HB, DHB, DHB), c3),
            pl.BlockSpec((1, MI), c2), pl.BlockSpec((1, MI), c2),
            pl.BlockSpec((L, L), c2), pl.BlockSpec((L, L), c2),
        ],
        out_specs=pl.BlockSpec((1, L, MI), lambda b, c: (b, c, 0)),
        out_shape=jax.ShapeDtypeStruct((B, T, MI), BF16),
        scratch_shapes=[pltpu.VMEM((L + 2 * SUB, MI), F32), pltpu.VMEM((HB, DHB, 2 * DHB), F32),
                        pltpu.VMEM((HB, SUB, LANE), F32)],
        compiler_params=_cparams(("parallel", "arbitrary")),
        name="mlstm",
    )(xo, gn, gt, bn, bt, w_conv, b_conv.reshape(1, MI), w_q.astype(BF16), w_k.astype(BF16),
      w_v.astype(BF16), mh_gain.reshape(1, MI), skip.reshape(1, MI), tri, tri.T)


def _memkv_kernel(m_ref, g_ref, wk_ref, wv_ref, k_ref, v_ref):
    hb = _rms(m_ref[0], g_ref[...]).astype(BF16)
    k_ref[0] = _dot(hb, wk_ref[...]).astype(BF16)
    v_ref[0] = _dot(hb, wv_ref[...]).astype(BF16)


def _memkv(mem, g_mem, w_k, w_v):
    B, M, _ = mem.shape
    c2 = lambda b: (0, 0)
    bs = lambda b: (b, 0, 0)
    return pl.pallas_call(
        _memkv_kernel,
        grid=(B,),
        in_specs=[pl.BlockSpec((1, M, D), bs), pl.BlockSpec((1, D), c2),
                  pl.BlockSpec((D, D), c2), pl.BlockSpec((D, D), c2)],
        out_specs=(pl.BlockSpec((1, M, D), bs), pl.BlockSpec((1, M, D), bs)),
        out_shape=(jax.ShapeDtypeStruct((B, M, D), BF16), jax.ShapeDtypeStruct((B, M, D), BF16)),
        compiler_params=_cparams(("parallel",)),
        name="memkv",
    )(mem, g_mem.reshape(1, D), w_k.astype(BF16), w_v.astype(BF16))


def _mid_kernel(x_ref, ya_ref, yb_ref, gab_ref, wua_ref, wub_ref, wo_ref, gx_ref, wq_ref, kx_ref, vx_ref,
                wox_ref, gf_ref, wrh_ref, wrl_ref, br_ref, ltri_ref,
                x2_ref, h3_ref, ti_ref, tw_ref, cnt_ref, run_ref, *, tm):
    @pl.when((pl.program_id(0) == 0) & (pl.program_id(1) == 0))
    def _():
        run_ref[...] = jnp.zeros(run_ref.shape, F32)

    a = _dot(ya_ref[0], wua_ref[...])
    b = _dot(yb_ref[0], wub_ref[...])
    merged = jax.nn.sigmoid(gab_ref[0, :, 0:D]) * a + jax.nn.sigmoid(gab_ref[0, :, D:2 * D]) * b
    x1 = x_ref[0] + _dot(merged.astype(BF16), wo_ref[...])
    q = (_dot(_rms(x1, gx_ref[...]).astype(BF16), wq_ref[...]) * (DHX ** -0.5)).astype(BF16)
    kx = kx_ref[0]
    vx = vx_ref[0]
    heads = []
    for h in range(HX):
        lo, hi = h * DHX, (h + 1) * DHX
        s = _dot_nt(q[:, lo:hi], kx[:, lo:hi])
        m = jnp.max(s, axis=-1, keepdims=True)
        e = jnp.exp(s - m)
        p = e * (1.0 / jnp.sum(e, axis=-1, keepdims=True))
        heads.append(_dot(p.astype(BF16), vx[:, lo:hi]).astype(BF16))
    x2 = x1 + _dot(jnp.concatenate(heads, axis=1), wox_ref[...])
    x2_ref[0] = x2
    h3 = _rms(x2, gf_ref[...])
    for s in range(SUB):
        h3_ref[:, s, :] = h3[:, s * LANE:(s + 1) * LANE]
    h_hi, h_lo = _split(h3)
    logits = (_dot(h_hi, wrh_ref[...]) + _dot(h_lo, wrh_ref[...]) + _dot(h_hi, wrl_ref[...])) + br_ref[...]
    lane = lax.broadcasted_iota(I32, (tm, LANE), 1)
    work = logits
    vals, idxs = [], []
    multi = jnp.zeros((tm, LANE), F32)
    for _ in range(TOPK):
        m = jnp.max(work, axis=-1, keepdims=True)
        first = jnp.min(jnp.where(work == m, lane, LANE), axis=-1, keepdims=True)
        pick = lane == first
        vals.append(m)
        idxs.append(first)
        multi = jnp.where(pick, 1.0, multi)
        work = jnp.where(pick, LOWEST, work)
    es = [jnp.exp(v - vals[0]) for v in vals]
    inv = 1.0 / (es[0] + es[1] + es[2] + es[3])
    tw = jnp.zeros((tm, LANE), F32)
    for k in range(TOPK):
        tw = jnp.where(lane == k, es[k] * inv, tw)
    tw_ref[0] = tw
    prior = _dot(ltri_ref[...], multi.astype(BF16)) + run_ref[0:1, :]
    ti = jnp.zeros((tm, LANE), I32)
    for k in range(TOPK):
        rk = jnp.sum(jnp.where(lane == idxs[k], prior, 0.0), axis=-1, keepdims=True)
        ti = jnp.where(lane == k, idxs[k], ti)
        ti = jnp.where(lane == TOPK + k, rk.astype(I32), ti)
    ti_ref[0] = ti
    run = run_ref[...] + jnp.sum(multi, axis=0, keepdims=True)
    run_ref[...] = run
    cnt_ref[...] = run


def _mid(x, ya, yb, gab, w_up_a, w_up_b, w_o_mix, g_x, w_q_x, kx, vx, w_o_x, g_ffn, w_router, b_router, tm):
    B, T, _ = x.shape
    M = kx.shape[1]
    wr = jnp.pad(w_router, ((0, 0), (0, LANE - NE)))
    wrh, wrl = _split(wr)
    br = jnp.concatenate([b_router, jnp.full((LANE - NE,), NEG, F32)]).reshape(1, LANE)
    ltri = (jnp.arange(tm)[None, :] < jnp.arange(tm)[:, None]).astype(BF16)
    c2 = lambda b, i: (0, 0)
    row = lambda b, i: (b, i, 0)
    bs = lambda b, i: (b, 0, 0)
    nt = T // tm
    return pl.pallas_call(
        functools.partial(_mid_kernel, tm=tm),
        grid=(B, nt),
        in_specs=[
            pl.BlockSpec((1, tm, D), row), pl.BlockSpec((1, tm, 512), row), pl.BlockSpec((1, tm, MI), row),
            pl.BlockSpec((1, tm, 2 * D), row),
            pl.BlockSpec((512, D), c2), pl.BlockSpec((MI, D), c2), pl.BlockSpec((D, D), c2),
            pl.BlockSpec((1, D), c2), pl.BlockSpec((D, D), c2),
            pl.BlockSpec((1, M, D), bs), pl.BlockSpec((1, M, D), bs),
            pl.BlockSpec((D, D), c2), pl.BlockSpec((1, D), c2),
            pl.BlockSpec((D, LANE), c2), pl.BlockSpec((D, LANE), c2), pl.BlockSpec((1, LANE), c2),
            pl.BlockSpec((tm, tm), c2),
        ],
        out_specs=(
            pl.BlockSpec((1, tm, D), row),
            pl.BlockSpec((tm, SUB, LANE), lambda b, i: (b * nt + i, 0, 0)),
            pl.BlockSpec((1, tm, LANE), row), pl.BlockSpec((1, tm, LANE), row),
            pl.BlockSpec((SUB, LANE), c2),
        ),
        out_shape=(
            jax.ShapeDtypeStruct((B, T, D), F32), jax.ShapeDtypeStruct((B * T, SUB, LANE), F32),
            jax.ShapeDtypeStruct((B, T, LANE), I32), jax.ShapeDtypeStruct((B, T, LANE), F32),
            jax.ShapeDtypeStruct((SUB, LANE), F32),
        ),
        scratch_shapes=[pltpu.VMEM((SUB, LANE), F32)],
        compiler_params=_cparams(("arbitrary", "arbitrary")),
        name="mid",
    )(x, ya, yb, gab, w_up_a.astype(BF16), w_up_b.astype(BF16), w_o_mix.astype(BF16), g_x.reshape(1, D),
      w_q_x.astype(BF16), kx, vx, w_o_x.astype(BF16), g_ffn.reshape(1, D), wrh, wrl, br, ltri)


def _expert_kernel(be_ref, bt_ref, na_ref, h3_hbm, wgu_ref, bgu_ref, wdn_ref, bdn_ref, out_ref, xbuf, sem):
    i = pl.program_id(0)
    nact = na_ref[0]

    def issue(blk, slot):
        base = blk * MBLK

        def body(r, carry):
            tok = bt_ref[base + r]
            pltpu.make_async_copy(h3_hbm.at[tok], xbuf.at[slot, r], sem.at[slot]).start()
            return carry

        lax.fori_loop(0, MBLK, body, 0)

    @pl.when(i == 0)
    def _():
        issue(0, 0)

    slot = i % 2

    @pl.when(i < nact)
    def _():
        pltpu.make_async_copy(h3_hbm.at[pl.ds(0, MBLK)], xbuf.at[slot], sem.at[slot]).wait()

        @pl.when(i + 1 < nact)
        def _():
            issue(i + 1, 1 - slot)

        x = jnp.concatenate([xbuf[slot, :, s, :] for s in range(SUB)], axis=1).astype(BF16)
        gu = _dot(x, wgu_ref[0]) + bgu_ref[0]
        g = jnp.minimum(gu[:, 0:DFF], LIMIT)
        u = jnp.clip(gu[:, DFF:2 * DFF], -LIMIT, LIMIT)
        act = (u + 1.0) * (g * jax.nn.sigmoid(ALPHA * g))
        y = _dot(act.astype(BF16), wdn_ref[0]) + bdn_ref[0]
        for s in range(SUB):
            out_ref[:, s, :] = y[:, s * LANE:(s + 1) * LANE]

    @pl.when(i >= nact)
    def _():
        out_ref[...] = jnp.zeros(out_ref.shape, F32)


def _experts(h3, blk_exp, buf_tok, nact, w_gu, b_gu, w_down, b_down):
    n_blk = blk_exp.shape[0]
    wsel = lambda i, be, bt, na: (be[i], 0, 0)
    grid_spec = pltpu.PrefetchScalarGridSpec(
        num_scalar_prefetch=3,
        grid=(n_blk,),
        in_specs=[
            pl.BlockSpec(memory_space=pl.ANY),
            pl.BlockSpec((1, D, 2 * DFF), wsel), pl.BlockSpec((1, 1, 2 * DFF), wsel),
            pl.BlockSpec((1, DFF, D), wsel), pl.BlockSpec((1, 1, D), wsel),
        ],
        out_specs=pl.BlockSpec((MBLK, SUB, LANE), lambda i, be, bt, na: (i, 0, 0)),
        scratch_shapes=[pltpu.VMEM((2, MBLK, SUB, LANE), F32), pltpu.SemaphoreType.DMA((2,))],
    )
    return pl.pallas_call(
        _expert_kernel,
        grid_spec=grid_spec,
        out_shape=jax.ShapeDtypeStruct((n_blk * MBLK, SUB, LANE), F32),
        compiler_params=_cparams(("arbitrary",)),
        name="experts",
    )(blk_exp, buf_tok, nact, h3, w_gu.astype(BF16), b_gu.reshape(NE, 1, 2 * DFF),
      w_down.astype(BF16), b_down.reshape(NE, 1, D))


def _combine_kernel(pos_ref, x2_ref, tw_ref, gfin_ref, y_hbm, out_ref, ybuf, sem, *, tc, final_norm):
    i = pl.program_id(0)
    nt = pl.num_programs(0)

    def issue(tile, slot):
        base = tile * tc * TOPK

        for k in range(TOPK):
            def body(r, carry, k=k):
                p = pos_ref[base + r * TOPK + k]
                pltpu.make_async_copy(y_hbm.at[p], ybuf.at[slot, k * tc + r], sem.at[slot]).start()
                return carry

            lax.fori_loop(0, tc, body, 0)

    @pl.when(i == 0)
    def _():
        issue(0, 0)

    slot = i % 2
    pltpu.make_async_copy(y_hbm.at[pl.ds(0, TOPK * tc)], ybuf.at[slot], sem.at[slot]).wait()

    @pl.when(i + 1 < nt)
    def _():
        issue(i + 1, 1 - slot)

    ws = [tw_ref[:, k:k + 1] for k in range(TOPK)]
    accs = []
    ssq = jnp.zeros((tc, LANE), F32)
    for s in range(SUB):
        a = x2_ref[:, s * LANE:(s + 1) * LANE]
        for k in range(TOPK):
            a = a + ws[k] * ybuf[slot, k * tc:(k + 1) * tc, s, :]
        accs.append(a)
        ssq = ssq + a * a
    if not final_norm:
        for s in range(SUB):
            out_ref[:, s * LANE:(s + 1) * LANE] = accs[s]
        return
    scale = lax.rsqrt(jnp.sum(ssq, axis=-1, keepdims=True) * (1.0 / D) + EPS)
    for s in range(SUB):
        out_ref[:, s * LANE:(s + 1) * LANE] = accs[s] * scale * gfin_ref[:, s * LANE:(s + 1) * LANE]


def _combine(pos_flat, x2, tw, g_final, ybuf, tc, final_norm):
    N = x2.shape[0]
    grid_spec = pltpu.PrefetchScalarGridSpec(
        num_scalar_prefetch=1,
        grid=(N // tc,),
        in_specs=[
            pl.BlockSpec((tc, D), lambda i, p: (i, 0)),
            pl.BlockSpec((tc, LANE), lambda i, p: (i, 0)),
            pl.BlockSpec((1, D), lambda i, p: (0, 0)),
            pl.BlockSpec(memory_space=pl.ANY),
        ],
        out_specs=pl.BlockSpec((tc, D), lambda i, p: (i, 0)),
        scratch_shapes=[pltpu.VMEM((2, TOPK * tc, SUB, LANE), F32), pltpu.SemaphoreType.DMA((2,))],
    )
    return pl.pallas_call(
        functools.partial(_combine_kernel, tc=tc, final_norm=final_norm),
        grid_spec=grid_spec,
        out_shape=jax.ShapeDtypeStruct((N, D), F32),
        compiler_params=_cparams(("arbitrary",)),
        name="combine",
    )(pos_flat, x2, tw, g_final.reshape(1, D), ybuf)


def _moe_slots(ti, counts, n_tok):
    top_idx = ti[:, 0:TOPK]
    rank = ti[:, TOPK:2 * TOPK]
    padded = (counts + MBLK - 1) // MBLK * MBLK
    p_end = jnp.cumsum(padded)
    p_start = p_end - padded
    pos = p_start[top_idx] + rank
    n_blk = n_tok * TOPK // MBLK + NE
    blk_start = jnp.arange(n_blk, dtype=I32) * MBLK
    blk_exp = jnp.minimum(jnp.sum(p_end[None, :] <= blk_start[:, None], axis=1), NE - 1).astype(I32)
    nact = (p_end[-1] // MBLK).astype(I32).reshape(1)
    tok = jnp.broadcast_to(jnp.arange(n_tok, dtype=I32)[:, None], (n_tok, TOPK))
    buf_tok = jnp.zeros((n_blk * MBLK,), I32).at[pos.reshape(-1)].set(tok.reshape(-1))
    return pos.reshape(-1).astype(I32), blk_exp, buf_tok, nact


def _layer(x, mem, g_mix, w_in, pe_cmp_k, pe_cmp_v, w_phi1_k, w_phi2_k, w_phi1_v, w_phi2_v,
           w_conv, b_conv, w_q_m, w_k_m, w_v_m, b_i, b_f, mh_gain, skip_m,
           w_up_a, w_up_b, w_o_mix, g_x, g_mem, w_q_x, w_k_x, w_v_x, w_o_x,
           g_ffn, w_router, b_router, w_gu, b_gu, w_down, b_down, g_final, final_norm):
    B, T, _ = x.shape
    kcmp, vcmp, kslc, kwin, tt, gt, gn, xo, gab = _inproj(x, g_mix, w_in, tm=256)
    kc, vct = _compress(kcmp, vcmp, pe_cmp_k, pe_cmp_v, w_phi1_k, w_phi2_k, w_phi1_v, w_phi2_v)
    ya = _nsa(tt, kslc, kwin, kc, vct, gt, tq=256)
    yb = _mlstm(xo, gn, gt, w_conv, b_conv, w_q_m, w_k_m, w_v_m, b_i, b_f, mh_gain, skip_m, L=256)
    kx, vx = _memkv(mem, g_mem, w_k_x, w_v_x)
    x2, h3, ti, tw, cnt = _mid(x, ya, yb, gab, w_up_a, w_up_b, w_o_mix, g_x, w_q_x, kx, vx, w_o_x,
                               g_ffn, w_router, b_router, tm=256)
    n_tok = B * T
    counts = cnt[0, 0:NE].astype(I32)
    pos, blk_exp, buf_tok, nact = _moe_slots(ti.reshape(n_tok, LANE), counts, n_tok)
    ybuf = _experts(h3, blk_exp, buf_tok, nact, w_gu, b_gu, w_down, b_down)
    out = _combine(pos, x2.reshape(n_tok, D), tw.reshape(n_tok, LANE), g_final, ybuf, tc=128,
                   final_norm=final_norm)
    return out.reshape(B, T, D)


def kernel(x, mem, g_mix, w_in, pe_cmp_k, pe_cmp_v, w_phi1_k, w_phi2_k, w_phi1_v, w_phi2_v, w_conv, b_conv, w_q_m, w_k_m, w_v_m, b_i, b_f, mh_gain, skip_m, w_up_a, w_up_b, w_o_mix, g_x, g_mem, w_q_x, w_k_x, w_v_x, w_o_x, g_ffn, w_router, b_router, w_gu, b_gu, w_down, b_down, g_final):
    layers = (g_mix, w_in, pe_cmp_k, pe_cmp_v, w_phi1_k, w_phi2_k, w_phi1_v, w_phi2_v, w_conv, b_conv,
              w_q_m, w_k_m, w_v_m, b_i, b_f, mh_gain, skip_m, w_up_a, w_up_b, w_o_mix, g_x, g_mem,
              w_q_x, w_k_x, w_v_x, w_o_x, g_ffn, w_router, b_router, w_gu, b_gu, w_down, b_down)
    depth = g_mix.shape[0]
    for l in range(depth):
        x = _layer(x, mem, *(w[l] for w in layers), g_final, final_norm=(l == depth - 1))
    return x
```

```python
import functools

import jax
import jax.numpy as jnp
from jax import lax
from jax.experimental import pallas as pl
from jax.experimental.pallas import tpu as pltpu

F32 = jnp.float32
BF16 = jnp.bfloat16
I32 = jnp.int32

D = 1024
EPS = 1e-5
NEG = -1e30
BIG = 1e30
LOWEST = -3e38

G = 2
HG = 4
DH = 64
CMP_STRIDE = 16
SLC = 64
NSEL = 16
WIN = 512
PHI = 128
NBP = 128
LOG2E = 1.4426950408889634
HB = 4
DHB = 128
MI = HB * DHB
HX = 4
DHX = D // HX
NE = 32
TOPK = 4
DFF = D
LIMIT = 7.0
ALPHA = 1.702
MBLK = 256

LANE = 128
SUB = 8
VMEM_LIMIT = 56 * 1024 * 1024

_O_Q = 0
_O_KCMP = 512
_O_VCMP = 640
_O_KSLC = 768
_O_VSLC = 896
_O_KWIN = 1024
_O_VWIN = 1152
_O_GNSA = 1280
_O_XM = 1304
_O_OPRE = 1816
_O_IPRE = 2328
_O_FPRE = 2332
_O_GA = 2336
_O_GB = 3360


def _cparams(sem):
    return pltpu.CompilerParams(dimension_semantics=sem, vmem_limit_bytes=VMEM_LIMIT)


def _rms(x, g):
    ms = jnp.mean(x * x, axis=-1, keepdims=True)
    return x * lax.rsqrt(ms + EPS) * g


def _dot(a, b):
    return jnp.dot(a, b, preferred_element_type=F32)


def _dot_nt(a, b):
    return lax.dot_general(a, b, (((1,), (1,)), ((), ())), preferred_element_type=F32)


def _split(x):
    hi = x.astype(BF16)
    lo = (x - hi.astype(F32)).astype(BF16)
    return hi, lo


def _inproj_kernel(x_ref, g_ref, wkn_ref, wt_ref, wgt_ref, wgn_ref, wxo_ref, wgab_ref,
                   kcmp_ref, vcmp_ref, kslc_ref, kwin_ref, tt_ref, gt_ref, gn_ref, xo_ref, gab_ref):
    hb = _rms(x_ref[0], g_ref[...]).astype(BF16)
    kn = _dot(hb, wkn_ref[...]).astype(BF16)
    kcmp_ref[0] = kn[:, 0:128]
    vcmp_ref[0] = kn[:, 128:256]
    kslc_ref[0] = kn[:, 256:384]
    kwin_ref[0] = kn[:, 384:512]
    tt_ref[0] = _dot_nt(wt_ref[...], hb).astype(BF16)
    gt_ref[0] = _dot_nt(wgt_ref[...], hb)
    gn_ref[0] = _dot(hb, wgn_ref[...])
    xo_ref[0] = _dot(hb, wxo_ref[...])
    gab_ref[0] = _dot(hb, wgab_ref[...])


def _inproj(x, g_mix, w_in, tm):
    B, T, _ = x.shape
    w = w_in
    wkn = jnp.concatenate([w[:, _O_KCMP:_O_VCMP], w[:, _O_VCMP:_O_KSLC], w[:, _O_KSLC:_O_VSLC],
                           w[:, _O_KWIN:_O_VWIN]], axis=1).astype(BF16)
    wt = jnp.concatenate([w[:, _O_Q:_O_KCMP] * (DH ** -0.5 * LOG2E), w[:, _O_VSLC:_O_KWIN],
                          w[:, _O_VWIN:_O_GNSA]], axis=1).T.astype(BF16)
    wg = jnp.concatenate([w[:, _O_GNSA:_O_XM], w[:, _O_IPRE:_O_GA]], axis=1)
    wgt = wg.T.astype(BF16)
    wgn = jnp.pad(wg, ((0, 0), (0, LANE - 32))).astype(BF16)
    wxo = w[:, _O_XM:_O_IPRE].astype(BF16)
    wgab = w[:, _O_GA:].astype(BF16)
    nt = T // tm
    row = lambda b, i: (b, i, 0)
    col = lambda b, i: (b, 0, i)
    full = lambda b, i: (0, 0)
    out_shape = (
        jax.ShapeDtypeStruct((B, T, 128), BF16), jax.ShapeDtypeStruct((B, T, 128), BF16),
        jax.ShapeDtypeStruct((B, T, 128), BF16), jax.ShapeDtypeStruct((B, T, 128), BF16),
        jax.ShapeDtypeStruct((B, 768, T), BF16), jax.ShapeDtypeStruct((B, 32, T), F32),
        jax.ShapeDtypeStruct((B, T, 128), F32), jax.ShapeDtypeStruct((B, T, 1024), F32),
        jax.ShapeDtypeStruct((B, T, 2048), F32),
    )
    return pl.pallas_call(
        _inproj_kernel,
        grid=(B, nt),
        in_specs=[
            pl.BlockSpec((1, tm, D), row), pl.BlockSpec((1, D), full),
            pl.BlockSpec((D, 512), full), pl.BlockSpec((768, D), full), pl.BlockSpec((32, D), full),
            pl.BlockSpec((D, 128), full), pl.BlockSpec((D, 1024), full), pl.BlockSpec((D, 2048), full),
        ],
        out_specs=(
            pl.BlockSpec((1, tm, 128), row), pl.BlockSpec((1, tm, 128), row),
            pl.BlockSpec((1, tm, 128), row), pl.BlockSpec((1, tm, 128), row),
            pl.BlockSpec((1, 768, tm), col), pl.BlockSpec((1, 32, tm), col),
            pl.BlockSpec((1, tm, 128), row), pl.BlockSpec((1, tm, 1024), row),
            pl.BlockSpec((1, tm, 2048), row),
        ),
        out_shape=out_shape,
        compiler_params=_cparams(("parallel", "parallel")),
        name="inproj",
    )(x, g_mix.reshape(1, D), wkn, wt, wgt, wgn, wxo, wgab)


def _compress_kernel(xk_ref, xv_ref, pek_ref, pev_ref, wklo_ref, wkhi_ref, wk2_ref,
                     wvlo_ref, wvhi_ref, wv2t_ref, kc_ref, vct_ref):
    n_sub = xk_ref.shape[1]

    def hidden(x, pe_ref, wlo_ref, whi_ref):
        wlo = wlo_ref[...]
        whi = whi_ref[...]
        c = _dot(pe_ref[0], wlo) + _dot(pe_ref[1], whi)
        a = _dot(x, wlo)
        b = _dot(x, whi)
        pre = a + pltpu.roll(b, n_sub - 1, 0) + c[0:1, :]
        return jax.nn.gelu(pre).astype(BF16)

    hk = hidden(xk_ref[0], pek_ref, wklo_ref, wkhi_ref)
    kc_ref[0] = _dot(hk, wk2_ref[...]).astype(BF16)
    hv = hidden(xv_ref[0], pev_ref, wvlo_ref, wvhi_ref)
    vct_ref[0] = _dot_nt(wv2t_ref[...], hv).astype(BF16)


def _compress(kcmp, vcmp, pe_k, pe_v, w1k, w2k, w1v, w2v):
    B, T, _ = kcmp.shape
    n_sub = T // CMP_STRIDE
    eye = jnp.eye(G, dtype=F32)

    def big1(w1):
        lo = w1[:CMP_STRIDE * DH].reshape(CMP_STRIDE, DH, PHI)
        hi = w1[CMP_STRIDE * DH:].reshape(CMP_STRIDE, DH, PHI)
        f = lambda a: jnp.einsum('rdp,gh->rgdhp', a, eye).reshape(CMP_STRIDE * G * DH, G * PHI).astype(BF16)
        return f(lo), f(hi)

    def big2(w2):
        return jnp.einsum('pd,gh->gphd', w2, eye).reshape(G * PHI, G * DH)

    def bigpe(pe):
        f = lambda a: jnp.broadcast_to(a[:, None, :], (CMP_STRIDE, G, DH)).reshape(1, CMP_STRIDE * G * DH)
        both = jnp.stack([f(pe[:CMP_STRIDE]), f(pe[CMP_STRIDE:])], axis=0)
        return jnp.broadcast_to(both, (2, SUB, CMP_STRIDE * G * DH)).astype(BF16)

    wklo, wkhi = big1(w1k)
    wvlo, wvhi = big1(w1v)
    wk2 = big2(w2k).astype(BF16)
    wv2t = big2(w2v).T.astype(BF16)
    kw = CMP_STRIDE * G * DH
    xk = kcmp.reshape(B, n_sub, kw)
    xv = vcmp.reshape(B, n_sub, kw)
    c2 = lambda b: (0, 0)
    c3 = lambda b: (0, 0, 0)
    bsel = lambda b: (b, 0, 0)
    return pl.pallas_call(
        _compress_kernel,
        grid=(B,),
        in_specs=[
            pl.BlockSpec((1, n_sub, kw), bsel), pl.BlockSpec((1, n_sub, kw), bsel),
            pl.BlockSpec((2, SUB, kw), c3), pl.BlockSpec((2, SUB, kw), c3),
            pl.BlockSpec((kw, G * PHI), c2), pl.BlockSpec((kw, G * PHI), c2), pl.BlockSpec((G * PHI, G * DH), c2),
            pl.BlockSpec((kw, G * PHI), c2), pl.BlockSpec((kw, G * PHI), c2), pl.BlockSpec((G * DH, G * PHI), c2),
        ],
        out_specs=(pl.BlockSpec((1, n_sub, G * DH), bsel), pl.BlockSpec((1, G * DH, n_sub), bsel)),
        out_shape=(jax.ShapeDtypeStruct((B, n_sub, G * DH), BF16),
                   jax.ShapeDtypeStruct((B, G * DH, n_sub), BF16)),
        compiler_params=_cparams(("parallel",)),
        name="compress",
    )(xk, xv, bigpe(pe_k), bigpe(pe_v), wklo, wkhi, wk2, wvlo, wvhi, wv2t)


def _topk_mask_axis0(score, k):
    n = score.shape[0]
    idx = lax.broadcasted_iota(I32, score.shape, 0)
    work = score
    for _ in range(k):
        m = jnp.max(work, axis=0, keepdims=True)
        first = jnp.min(jnp.where(work == m, idx, n), axis=0, keepdims=True)
        work = jnp.where(idx == first, LOWEST, work)
    return work < 0.5 * LOWEST


def _nsa_kernel(qt_ref, kslc_ref, vslct_ref, kwin_ref, vwint_ref, kc_ref, vct_ref, gt_ref,
                at_ref, e_ref, out_ref, acc_ref, accw_ref, *, tq):
    T = kslc_ref.shape[1]
    n_sub = T // CMP_STRIDE
    nsel = min(NSEL, T // SLC)
    i = pl.program_id(1)
    q0 = i * tq
    t_row = q0 + lax.broadcasted_iota(I32, (1, tq), 1)
    kcv = kc_ref[0]
    cmp_end = lax.broadcasted_iota(I32, (n_sub, 1), 0) * CMP_STRIDE + (2 * CMP_STRIDE - 1)
    bias_c = jnp.where(cmp_end <= t_row, 0.0, NEG)
    col_ok = t_row >= 2 * CMP_STRIDE - 1
    k_loc = lax.broadcasted_iota(I32, (tq, tq), 0)
    q_loc = lax.broadcasted_iota(I32, (tq, tq), 1)
    tri_bias = jnp.where(k_loc <= q_loc, 0.0, NEG)
    prev_bias = jnp.where(k_loc > q_loc + jnp.where(i > 0, 0, tq), 0.0, NEG)
    blk = lax.broadcasted_iota(I32, (NBP, tq), 0)
    cur = t_row // SLC
    at = at_ref[...]
    zeros_half = jnp.zeros((DH, tq), BF16)
    ones_k = jnp.ones((2 * SUB, tq), BF16)
    pieces = []
    for g in range(G):
        glo, ghi = g * DH, (g + 1) * DH
        qps = []
        for h in range(HG):
            r0 = (g * HG + h) * DH
            qh = qt_ref[0, r0:r0 + DH, :]
            qps.append(jnp.concatenate([qh, zeros_half] if g == 0 else [zeros_half, qh], axis=0))
        qp_all = jnp.concatenate(qps, axis=1)

        vc_aug = jnp.concatenate([vct_ref[0, glo:ghi, :], jnp.ones((2 * SUB, n_sub), BF16), at], axis=0)
        s_c = _dot(kcv, qp_all)
        imp = jnp.zeros((NBP, tq), F32)
        o_cmp = []
        for h in range(HG):
            s_h = s_c[:, h * tq:(h + 1) * tq] + bias_c
            p = jnp.exp2(s_h - jnp.max(s_h, axis=0, keepdims=True)).astype(BF16)
            r = _dot(vc_aug, p)
            inv = jnp.where(col_ok, 1.0 / r[DH:DH + 1], 0.0)
            o_cmp.append(r[0:DH] * inv)
            imp = imp + r[DH + 2 * SUB:] * inv
        forced = jnp.where(blk == 0, 0.0, jnp.where(blk == cur, 0.0, jnp.where(blk == cur - 1, 0.0, NEG)))
        free = jnp.where(blk <= cur, jnp.where(forced < 0.0, imp, NEG), NEG)
        picked = _topk_mask_axis0(free, nsel - 3)
        selbias = jnp.where(picked, 0.0, forced).astype(BF16)
        q_aug = jnp.concatenate([qp_all, jnp.concatenate([selbias] * HG, axis=1)], axis=0)
        def chunk(k_ref, vt_ref, acc, rhs, c, ms, bias, glo=glo, ghi=ghi):
            k0 = pl.multiple_of(c * tq, tq)
            lhs = k_ref[0, pl.ds(k0, tq), :]
            if rhs.shape[0] == 2 * LANE:
                lhs = jnp.concatenate([lhs, e_ref[pl.ds(k0, tq), :]], axis=1)
            v_aug = jnp.concatenate([vt_ref[0, glo:ghi, pl.ds(k0, tq)], ones_k], axis=0)
            s = _dot(lhs, rhs)
            new_ms = []
            for h in range(HG):
                s_h = s[:, h * tq:(h + 1) * tq]
                if bias is not None:
                    s_h = s_h + bias
                m_new = jnp.maximum(ms[h], jnp.max(s_h, axis=0, keepdims=True))
                alpha = jnp.exp2(ms[h] - m_new)
                p = jnp.exp2(s_h - m_new).astype(BF16)
                acc[h] = alpha * acc[h] + _dot(v_aug, p)
                new_ms.append(m_new)
            return tuple(new_ms)

        ms0 = tuple(jnp.full((1, tq), NEG, F32) for _ in range(HG))
        acc_ref[...] = jnp.zeros(acc_ref.shape, F32)
        ms = lax.fori_loop(0, i, lambda c, ms, q_aug=q_aug: chunk(kslc_ref, vslct_ref, acc_ref, q_aug, c, ms, None),
                           ms0)
        chunk(kslc_ref, vslct_ref, acc_ref, q_aug, i, ms, tri_bias)
        accw_ref[...] = jnp.zeros(accw_ref.shape, F32)
        ms = chunk(kwin_ref, vwint_ref, accw_ref, qp_all, i, ms0, tri_bias)
        chunk(kwin_ref, vwint_ref, accw_ref, qp_all, jnp.maximum(i - 1, 0), ms, prev_bias)
        for h in range(HG):
            accv = acc_ref[h]
            o_slc = accv[0:DH] * (1.0 / accv[DH:DH + 1])
            accv = accw_ref[h]
            o_win = accv[0:DH] * (1.0 / accv[DH:DH + 1])
            gi = (g * HG + h) * 3
            sg = jax.nn.sigmoid(gt_ref[0, gi:gi + 3, :])
            pieces.append(sg[0:1] * o_cmp[h] + sg[1:2] * o_slc + sg[2:3] * o_win)
    yt = jnp.concatenate(pieces, axis=0)
    out_ref[0] = yt.T.astype(BF16)


def _nsa(tt, kslc, kwin, kc, vct, gt, tq):
    B, T, _ = kslc.shape
    n_sub = T // CMP_STRIDE
    nb = T // SLC
    assert nb <= NBP and T % tq == 0
    assert tq == WIN
    cidx = jnp.arange(n_sub)[None, :]
    bidx = jnp.arange(NBP)[:, None]
    off = cidx - (SLC // CMP_STRIDE) * bidx
    at = (jnp.where((off >= -1) & (off <= 3), 1.0, 0.0)
          + jnp.where((off >= 0) & (off <= 2), 1.0, 0.0))
    at = jnp.where((cidx < n_sub - 1) & (bidx < nb), at, 0.0).astype(BF16)
    e = (jnp.arange(T)[:, None] // SLC == jnp.arange(NBP)[None, :]).astype(BF16)
    nt = T // tq
    kern = functools.partial(_nsa_kernel, tq=tq)
    bfull = lambda b, i: (b, 0, 0)
    return pl.pallas_call(
        kern,
        grid=(B, nt),
        in_specs=[
            pl.BlockSpec((1, 512, tq), lambda b, i: (b, 0, i)),
            pl.BlockSpec((1, T, 128), bfull),
            pl.BlockSpec((1, 128, T), lambda b, i: (b, 4, 0)),
            pl.BlockSpec((1, T, 128), bfull),
            pl.BlockSpec((1, 128, T), lambda b, i: (b, 5, 0)),
            pl.BlockSpec((1, n_sub, 128), bfull),
            pl.BlockSpec((1, 128, n_sub), bfull),
            pl.BlockSpec((1, 32, tq), lambda b, i: (b, 0, i)),
            pl.BlockSpec((NBP, n_sub), lambda b, i: (0, 0)),
            pl.BlockSpec((T, NBP), lambda b, i: (0, 0)),
        ],
        out_specs=pl.BlockSpec((1, tq, 512), lambda b, i: (b, i, 0)),
        out_shape=jax.ShapeDtypeStruct((B, T, 512), BF16),
        scratch_shapes=[pltpu.VMEM((HG, DH + 2 * SUB, tq), F32), pltpu.VMEM((HG, DH + 2 * SUB, tq), F32)],
        compiler_params=_cparams(("parallel", "parallel")),
        name="nsa",
    )(tt, kslc, tt, kwin, tt, kc, vct, gt, at, e)


def _log_sigmoid(x):
    return jnp.minimum(x, 0.0) - jnp.log(1.0 + jnp.exp(-jnp.abs(x)))


def _mlstm_kernel(xo_ref, gn_ref, gt_ref, bn_ref, bt_ref, wconv_ref, bconv_ref, wq_ref, wk_ref, wv_ref,
                  gain_ref, skip_ref, tri_ref, trit_ref, out_ref, xbuf, caug, mst, *, L):
    c = pl.program_id(1)

    @pl.when(c == 0)
    def _():
        xbuf[...] = jnp.zeros(xbuf.shape, F32)
        caug[...] = jnp.zeros(caug.shape, F32)
        mst[...] = jnp.zeros(mst.shape, F32)

    xbuf[0:SUB, :] = xbuf[L:L + SUB, :]
    xbuf[SUB:SUB + L, :] = xo_ref[0, :, 0:MI]
    conv = bconv_ref[...] + wconv_ref[3:4, :] * xbuf[SUB:SUB + L, :]
    for r in range(3):
        s = 3 - r
        conv = conv + wconv_ref[r:r + 1, :] * xbuf[SUB - s:SUB - s + L, :]
    xc = conv * jax.nn.sigmoid(conv)

    gn = gn_ref[0] + bn_ref[...]
    gt = gt_ref[0] + bt_ref[...]
    lf_hi, lf_lo = _split(_log_sigmoid(gn))
    tri = tri_ref[...]
    b_col_all = _dot(tri, lf_hi) + _dot(tri, lf_lo)
    lft_hi, lft_lo = _split(_log_sigmoid(gt))
    trit = trit_ref[...]
    b_row_all = _dot(lft_hi, trit) + _dot(lft_lo, trit)
    ti = lax.broadcasted_iota(I32, (L, L), 0)
    si = lax.broadcasted_iota(I32, (L, L), 1)
    causal = si <= ti
    one_col = jnp.where(lax.broadcasted_iota(I32, (L, DHB), 1) == 0, 1.0, 0.0).astype(BF16)

    outs = []
    for h in range(HB):
        lo, hi = h * DHB, (h + 1) * DHB
        xch = xc[:, lo:hi]
        xcb = xch.astype(BF16)
        q = _dot(xcb, wq_ref[h]).astype(BF16)
        k = _dot(xcb, wk_ref[h]) * (DHB ** -0.5)
        v = _dot(xo_ref[0, :, lo:hi].astype(BF16), wv_ref[h]).astype(BF16)
        v_aug = jnp.concatenate([v, one_col], axis=1)
        bc = b_col_all[:, 28 + h:29 + h]
        lic = gn[:, 24 + h:25 + h]
        br = b_row_all[28 + h:29 + h, :]
        lir = gt[24 + h:25 + h, :]
        m_prev = mst[h, 0:1, 0:1]
        log_d = jnp.where(causal, bc - br + lir, NEG)
        m_inter = bc + m_prev
        m_t = jnp.maximum(jnp.max(log_d, axis=1, keepdims=True), m_inter)
        dmat = jnp.exp(log_d - m_t)
        s = (_dot_nt(q, k.astype(BF16)) * dmat).astype(BF16)
        inter = jnp.exp(m_inter - m_t)
        c_old = caug[h]
        r = _dot(s, v_aug) + inter * _dot(q, c_old.astype(BF16))
        num = r[:, 0:DHB]
        den = r[:, DHB:DHB + 1]
        hval = num / jnp.maximum(jnp.abs(den), jnp.exp(-m_t))
        b_last = bc[L - 1:L, :]
        log_w = b_last - bc + lic
        m_new = jnp.maximum(b_last + m_prev, jnp.max(log_w, axis=0, keepdims=True))
        w = jnp.exp(log_w - m_new)
        decay = jnp.exp(b_last + m_prev - m_new)
        kwt = (k * w).T.astype(BF16)
        caug[h] = decay * c_old + _dot(kwt, v_aug)
        mst[h] = jnp.broadcast_to(m_new, (SUB, LANE))
        o = jax.nn.sigmoid(xo_ref[0, :, MI + lo:MI + hi]) * hval
        mu = jnp.mean(o, axis=-1, keepdims=True)
        var = jnp.mean((o - mu) ** 2, axis=-1, keepdims=True)
        hn = (o - mu) * lax.rsqrt(var + EPS) * gain_ref[:, lo:hi]
        outs.append(hn + skip_ref[:, lo:hi] * xch)
    out_ref[0] = jnp.concatenate(outs, axis=1).astype(BF16)


def _mlstm(xo, gn, gt, w_conv, b_conv, w_q, w_k, w_v, b_i, b_f, mh_gain, skip, L):
    B, T, _ = xo.shape
    bias = jnp.concatenate([jnp.zeros((24,), F32), b_i, b_f])
    bn = jnp.pad(bias, (0, LANE - 32)).reshape(1, LANE)
    bt = bias.reshape(32, 1)
    tri = (jnp.arange(L)[None, :] <= jnp.arange(L)[:, None]).astype(BF16)
    c2 = lambda b, c: (0, 0)
    c3 = lambda b, c: (0, 0, 0)
    return pl.pallas_call(
        functools.partial(_mlstm_kernel, L=L),
        grid=(B, T // L),
        in_specs=[
            pl.BlockSpec((1, L, 2 * MI), lambda b, c: (b, c, 0)),
            pl.BlockSpec((1, L, LANE), lambda b, c: (b, c, 0)),
            pl.BlockSpec((1, 32, L), lambda b, c: (b, 0, c)),
            pl.BlockSpec((1, LANE), c2), pl.BlockSpec((32, 1), c2),
            pl.BlockSpec((4, MI), c2), pl.BlockSpec((1, MI), c2),
            pl.BlockSpec((HB, DHB, DHB), c3), pl.BlockSpec((HB, DHB, DHB), c3), pl.BlockSpec((HB, DHB, DHB), c3),
            pl.BlockSpec((1, MI), c2), pl.BlockSpec((1, MI), c2),
            pl.BlockSpec((L, L), c2), pl.BlockSpec((L, L), c2),
        ],
        out_specs=pl.BlockSpec((1, L, MI), lambda b, c: (b, c, 0)),
        out_shape=jax.ShapeDtypeStruct((B, T, MI), BF16),
        scratch_shapes=[pltpu.VMEM((L + 2 * SUB, MI), F32), pltpu.VMEM((HB, DHB, 2 * DHB), F32),
                        pltpu.VMEM((HB, SUB, LANE), F32)],
        compiler_params=_cparams(("parallel", "arbitrary")),
        name="mlstm",
    )(xo, gn, gt, bn, bt, w_conv, b_conv.reshape(1, MI), w_q.astype(BF16), w_k.astype(BF16),
      w_v.astype(BF16), mh_gain.reshape(1, MI), skip.reshape(1, MI), tri, tri.T)


def _memkv_kernel(m_ref, g_ref, wk_ref, wv_ref, k_ref, v_ref):
    hb = _rms(m_ref[0], g_ref[...]).astype(BF16)
    k_ref[0] = _dot(hb, wk_ref[...]).astype(BF16)
    v_ref[0] = _dot(hb, wv_ref[...]).astype(BF16)


def _memkv(mem, g_mem, w_k, w_v):
    B, M, _ = mem.shape
    c2 = lambda b: (0, 0)
    bs = lambda b: (b, 0, 0)
    return pl.pallas_call(
        _memkv_kernel,
        grid=(B,),
        in_specs=[pl.BlockSpec((1, M, D), bs), pl.BlockSpec((1, D), c2),
                  pl.BlockSpec((D, D), c2), pl.BlockSpec((D, D), c2)],
        out_specs=(pl.BlockSpec((1, M, D), bs), pl.BlockSpec((1, M, D), bs)),
        out_shape=(jax.ShapeDtypeStruct((B, M, D), BF16), jax.ShapeDtypeStruct((B, M, D), BF16)),
        compiler_params=_cparams(("parallel",)),
        name="memkv",
    )(mem, g_mem.reshape(1, D), w_k.astype(BF16), w_v.astype(BF16))


def _mid_kernel(x_ref, ya_ref, yb_ref, gab_ref, wua_ref, wub_ref, wo_ref, gx_ref, wq_ref, kx_ref, vx_ref,
                wox_ref, gf_ref, wrh_ref, wrl_ref, br_ref,
                x2_ref, h3_ref, ti_ref, tw_ref, cnt_ref, *, tm):
    a = _dot(ya_ref[0], wua_ref[...])
    b = _dot(yb_ref[0], wub_ref[...])
    merged = jax.nn.sigmoid(gab_ref[0, :, 0:D]) * a + jax.nn.sigmoid(gab_ref[0, :, D:2 * D]) * b
    x1 = x_ref[0] + _dot(merged.astype(BF16), wo_ref[...])
    q = (_dot(_rms(x1, gx_ref[...]).astype(BF16), wq_ref[...]) * (DHX ** -0.5)).astype(BF16)
    kx = kx_ref[0]
    vx = vx_ref[0]
    heads = []
    for h in range(HX):
        lo, hi = h * DHX, (h + 1) * DHX
        s = _dot_nt(q[:, lo:hi], kx[:, lo:hi])
        m = jnp.max(s, axis=-1, keepdims=True)
        e = jnp.exp(s - m)
        p = e * (1.0 / jnp.sum(e, axis=-1, keepdims=True))
        heads.append(_dot(p.astype(BF16), vx[:, lo:hi]).astype(BF16))
    x2 = x1 + _dot(jnp.concatenate(heads, axis=1), wox_ref[...])
    x2_ref[0] = x2
    h3 = _rms(x2, gf_ref[...])
    h3_ref[0] = h3
    h_hi, h_lo = _split(h3)
    logits = (_dot(h_hi, wrh_ref[...]) + _dot(h_lo, wrh_ref[...]) + _dot(h_hi, wrl_ref[...])) + br_ref[...]
    lane = lax.broadcasted_iota(I32, (tm, LANE), 1)
    work = logits
    vals, idxs = [], []
    multi = jnp.zeros((tm, LANE), F32)
    for _ in range(TOPK):
        m = jnp.max(work, axis=-1, keepdims=True)
        first = jnp.min(jnp.where(work == m, lane, LANE), axis=-1, keepdims=True)
        pick = lane == first
        vals.append(m)
        idxs.append(first)
        multi = jnp.where(pick, 1.0, multi)
        work = jnp.where(pick, LOWEST, work)
    es = [jnp.exp(v - vals[0]) for v in vals]
    inv = 1.0 / (es[0] + es[1] + es[2] + es[3])
    tw = jnp.zeros((tm, LANE), F32)
    for k in range(TOPK):
        tw = jnp.where(lane == k, es[k] * inv, tw)
    tw_ref[0] = tw
    ti = jnp.zeros((tm, LANE), I32)
    for k in range(TOPK):
        ti = jnp.where(lane == k, idxs[k], ti)
    ti_ref[0] = ti
    cnt_ref[0] = jnp.broadcast_to(jnp.sum(multi, axis=0, keepdims=True), (SUB, LANE))


def _mid(x, ya, yb, gab, w_up_a, w_up_b, w_o_mix, g_x, w_q_x, kx, vx, w_o_x, g_ffn, w_router, b_router, tm):
    B, T, _ = x.shape
    M = kx.shape[1]
    wr = jnp.pad(w_router, ((0, 0), (0, LANE - NE)))
    wrh, wrl = _split(wr)
    br = jnp.concatenate([b_router, jnp.full((LANE - NE,), NEG, F32)]).reshape(1, LANE)
    c2 = lambda b, i: (0, 0)
    row = lambda b, i: (b, i, 0)
    bs = lambda b, i: (b, 0, 0)
    nt = T // tm
    return pl.pallas_call(
        functools.partial(_mid_kernel, tm=tm),
        grid=(B, nt),
        in_specs=[
            pl.BlockSpec((1, tm, D), row), pl.BlockSpec((1, tm, 512), row), pl.BlockSpec((1, tm, MI), row),
            pl.BlockSpec((1, tm, 2 * D), row),
            pl.BlockSpec((512, D), c2), pl.BlockSpec((MI, D), c2), pl.BlockSpec((D, D), c2),
            pl.BlockSpec((1, D), c2), pl.BlockSpec((D, D), c2),
            pl.BlockSpec((1, M, D), bs), pl.BlockSpec((1, M, D), bs),
            pl.BlockSpec((D, D), c2), pl.BlockSpec((1, D), c2),
            pl.BlockSpec((D, LANE), c2), pl.BlockSpec((D, LANE), c2), pl.BlockSpec((1, LANE), c2),
        ],
        out_specs=(
            pl.BlockSpec((1, tm, D), row), pl.BlockSpec((1, tm, D), row),
            pl.BlockSpec((1, tm, LANE), row), pl.BlockSpec((1, tm, LANE), row),
            pl.BlockSpec((1, SUB, LANE), lambda b, i: (b * nt + i, 0, 0)),
        ),
        out_shape=(
            jax.ShapeDtypeStruct((B, T, D), F32), jax.ShapeDtypeStruct((B, T, D), F32),
            jax.ShapeDtypeStruct((B, T, LANE), I32), jax.ShapeDtypeStruct((B, T, LANE), F32),
            jax.ShapeDtypeStruct((B * nt, SUB, LANE), F32),
        ),
        compiler_params=_cparams(("parallel", "parallel")),
        name="mid",
    )(x, ya, yb, gab, w_up_a.astype(BF16), w_up_b.astype(BF16), w_o_mix.astype(BF16), g_x.reshape(1, D),
      w_q_x.astype(BF16), kx, vx, w_o_x.astype(BF16), g_ffn.reshape(1, D), wrh, wrl, br)


def _local_rows(ti, ltri, utri):
    tm = ti.shape[0]
    lane = lax.broadcasted_iota(I32, (tm, LANE), 1)
    hots = [lane == ti[:, k:k + 1] for k in range(TOPK)]
    multi = jnp.zeros((tm, LANE), F32)
    for hot in hots:
        multi = jnp.where(hot, 1.0, multi)
    prior = _dot(ltri, multi.astype(BF16))
    cnt = jnp.broadcast_to(jnp.sum(multi, axis=0, keepdims=True), (SUB, LANE))
    loc = _dot(cnt.astype(BF16), utri)[0:1]
    base = prior + loc
    return [jnp.sum(jnp.where(hot, base, 0.0), axis=-1, keepdims=True) for hot in hots]


def _run_copies(cnt_ref, gs_ref, tile, make_copy):
    def body(e, local):
        n = cnt_ref[tile * NE + e]

        @pl.when(n > 0)
        def _():
            make_copy(local, gs_ref[tile * NE + e], n).start()

        return local + n

    lax.fori_loop(0, NE, body, 0)


def _dispatch_kernel(cnt_ref, gs_ref, ps_ref, pn_ref, na_ref, h3_ref, ti_ref, ltri_ref, utri_ref, xbuf_hbm,
                     xl, zbuf, sem, zsem, *, tm, nt, n_blk):
    j = pl.program_id(0)
    slot = j % 2
    rows = TOPK * tm

    def wait_tile(s):
        pltpu.make_async_copy(xl.at[s], xbuf_hbm.at[pl.ds(0, rows)], sem.at[s]).wait()

    def pad_copy(e):
        return pltpu.make_async_copy(zbuf.at[pl.ds(0, pn_ref[e])], xbuf_hbm.at[pl.ds(ps_ref[e], pn_ref[e])], zsem)

    def tail_copy(b):
        return pltpu.make_async_copy(zbuf, xbuf_hbm.at[pl.ds(pl.multiple_of(b * MBLK, MBLK), MBLK)], zsem)

    def for_tail_blocks(fn):
        def body(b, carry):
            fn(b)
            return carry

        lax.fori_loop(na_ref[0], n_blk, body, 0)

    @pl.when(j >= 2)
    def _():
        wait_tile(slot)

    @pl.when(j == 0)
    def _():
        zbuf[...] = jnp.zeros(zbuf.shape, F32)
        for e in range(NE):
            @pl.when(pn_ref[e] > 0)
            def _(e=e):
                pad_copy(e).start()
        for_tail_blocks(lambda b: tail_copy(b).start())

    lrs = _local_rows(ti_ref[...], ltri_ref[...], utri_ref[...])
    lane = lax.broadcasted_iota(I32, (tm, LANE), 1)
    lmat = jnp.zeros((tm, LANE), F32)
    for k in range(TOPK):
        lmat = jnp.where(lane == k, lrs[k], lmat)
    lt = lmat.T
    rio = lax.broadcasted_iota(I32, (rows, 1), 0).astype(F32)
    st = jnp.zeros((rows, tm), F32)
    for k in range(TOPK):
        st = jnp.where(rio == lt[k:k + 1, :], 1.0, st)
    xv = _dot(st.astype(BF16), h3_ref[...].astype(BF16))
    for s in range(SUB):
        xl[slot, :, s, :] = xv[:, s * LANE:(s + 1) * LANE]
    _run_copies(cnt_ref, gs_ref, j,
                lambda l, g, n: pltpu.make_async_copy(xl.at[slot, pl.ds(l, n)], xbuf_hbm.at[pl.ds(g, n)],
                                                      sem.at[slot]))

    @pl.when(j == nt - 1)
    def _():
        wait_tile(slot)
        if nt >= 2:
            wait_tile(1 - slot)
        for e in range(NE):
            @pl.when(pn_ref[e] > 0)
            def _(e=e):
                pad_copy(e).wait()
        for_tail_blocks(lambda b: tail_copy(b).wait())


def _dispatch(h3, ti, plan, n_slots, tm):
    n_tok = h3.shape[0]
    nt = n_tok // tm
    ltri = (jnp.arange(tm)[None, :] < jnp.arange(tm)[:, None]).astype(BF16)
    utri = (jnp.arange(LANE)[:, None] < jnp.arange(LANE)[None, :]).astype(BF16)
    c2 = lambda j, *_: (0, 0)
    grid_spec = pltpu.PrefetchScalarGridSpec(
        num_scalar_prefetch=5,
        grid=(nt,),
        in_specs=[
            pl.BlockSpec((tm, D), lambda j, *_: (j, 0)), pl.BlockSpec((tm, LANE), lambda j, *_: (j, 0)),
            pl.BlockSpec((tm, tm), c2), pl.BlockSpec((LANE, LANE), c2),
        ],
        out_specs=pl.BlockSpec(memory_space=pl.ANY),
        scratch_shapes=[pltpu.VMEM((2, TOPK * tm, SUB, LANE), F32), pltpu.VMEM((MBLK, SUB, LANE), F32),
                        pltpu.SemaphoreType.DMA((2,)), pltpu.SemaphoreType.DMA(())],
    )
    return pl.pallas_call(
        functools.partial(_dispatch_kernel, tm=tm, nt=nt, n_blk=n_slots // MBLK),
        grid_spec=grid_spec,
        out_shape=jax.ShapeDtypeStruct((n_slots, SUB, LANE), F32),
        compiler_params=_cparams(("arbitrary",)),
        name="dispatch",
    )(plan["cnt"], plan["gstart"], plan["pad_start"], plan["pad_len"], plan["nact"], h3, ti, ltri, utri)


def _expert_kernel(be_ref, na_ref, x_ref, wgu_ref, bgu_ref, wdn_ref, bdn_ref, out_ref, wgu_bf, wdn_bf):
    i = pl.program_id(0)
    nact = na_ref[0]

    @pl.when(i < nact)
    def _():
        @pl.when(jnp.logical_or(i == 0, be_ref[i] != be_ref[jnp.maximum(i - 1, 0)]))
        def _():
            wgu_bf[...] = wgu_ref[0].astype(BF16)
            wdn_bf[...] = wdn_ref[0].astype(BF16)

        x = jnp.concatenate([x_ref[:, s, :] for s in range(SUB)], axis=1).astype(BF16)
        gu = _dot(x, wgu_bf[...]) + bgu_ref[0]
        g = jnp.minimum(gu[:, 0:DFF], LIMIT)
        u = jnp.clip(gu[:, DFF:2 * DFF], -LIMIT, LIMIT)
        act = (u + 1.0) * (g * jax.nn.sigmoid(ALPHA * g))
        y = _dot(act.astype(BF16), wdn_bf[...]) + bdn_ref[0]
        for s in range(SUB):
            out_ref[:, s, :] = y[:, s * LANE:(s + 1) * LANE]

    @pl.when(i >= nact)
    def _():
        out_ref[...] = jnp.zeros(out_ref.shape, F32)


def _experts(xbuf, plan, w_gu, b_gu, w_down, b_down):
    n_blk = plan["blk_exp"].shape[0]
    wsel = lambda i, be, na: (be[i], 0, 0)
    grid_spec = pltpu.PrefetchScalarGridSpec(
        num_scalar_prefetch=2,
        grid=(n_blk,),
        in_specs=[
            pl.BlockSpec((MBLK, SUB, LANE), lambda i, be, na: (jnp.minimum(i, na[0] - 1), 0, 0)),
            pl.BlockSpec((1, D, 2 * DFF), wsel), pl.BlockSpec((1, 1, 2 * DFF), wsel),
            pl.BlockSpec((1, DFF, D), wsel), pl.BlockSpec((1, 1, D), wsel),
        ],
        out_specs=pl.BlockSpec((MBLK, SUB, LANE), lambda i, be, na: (i, 0, 0)),
        scratch_shapes=[pltpu.VMEM((D, 2 * DFF), BF16), pltpu.VMEM((DFF, D), BF16)],
    )
    return pl.pallas_call(
        _expert_kernel,
        grid_spec=grid_spec,
        out_shape=jax.ShapeDtypeStruct((n_blk * MBLK, SUB, LANE), F32),
        compiler_params=_cparams(("arbitrary",)),
        name="experts",
    )(plan["blk_exp"], plan["nact"], xbuf, w_gu, b_gu.reshape(NE, 1, 2 * DFF), w_down, b_down.reshape(NE, 1, D))


def _combine_kernel(cnt_ref, gs_ref, x2_ref, ti_ref, tw_ref, gfin_ref, ltri_ref, utri_ref, y_hbm, out_ref,
                    yl, sem, *, tm, nt, final_norm):
    j = pl.program_id(0)
    slot = j % 2
    rows = TOPK * tm

    def issue(tile, s):
        _run_copies(cnt_ref, gs_ref, tile,
                    lambda l, g, n: pltpu.make_async_copy(y_hbm.at[pl.ds(g, n)], yl.at[s, pl.ds(l, n)], sem.at[s]))

    @pl.when(j == 0)
    def _():
        issue(0, 0)

    pltpu.make_async_copy(y_hbm.at[pl.ds(0, rows)], yl.at[slot], sem.at[slot]).wait()

    @pl.when(j + 1 < nt)
    def _():
        issue(j + 1, 1 - slot)

    lrs = _local_rows(ti_ref[...], ltri_ref[...], utri_ref[...])
    rio = lax.broadcasted_iota(I32, (1, rows), 1).astype(F32)
    sw = jnp.zeros((tm, rows), F32)
    for k in range(TOPK):
        sw = jnp.where(rio == lrs[k], tw_ref[:, k:k + 1], sw)
    yv = jnp.concatenate([yl[slot, :, s, :] for s in range(SUB)], axis=1)
    s_hi, s_lo = _split(sw)
    y_hi, y_lo = _split(yv)
    acc = x2_ref[...] + ((_dot(s_hi, y_hi) + _dot(s_lo, y_hi)) + _dot(s_hi, y_lo))
    if final_norm:
        acc = _rms(acc, gfin_ref[...])
    out_ref[...] = acc


def _combine(plan, x2, ti, tw, g_final, ybuf, tm, final_norm):
    n_tok = x2.shape[0]
    nt = n_tok // tm
    ltri = (jnp.arange(tm)[None, :] < jnp.arange(tm)[:, None]).astype(BF16)
    utri = (jnp.arange(LANE)[:, None] < jnp.arange(LANE)[None, :]).astype(BF16)
    c2 = lambda j, *_: (0, 0)
    tile = lambda j, *_: (j, 0)
    grid_spec = pltpu.PrefetchScalarGridSpec(
        num_scalar_prefetch=2,
        grid=(nt,),
        in_specs=[
            pl.BlockSpec((tm, D), tile), pl.BlockSpec((tm, LANE), tile), pl.BlockSpec((tm, LANE), tile),
            pl.BlockSpec((1, D), c2), pl.BlockSpec((tm, tm), c2), pl.BlockSpec((LANE, LANE), c2),
            pl.BlockSpec(memory_space=pl.ANY),
        ],
        out_specs=pl.BlockSpec((tm, D), tile),
        scratch_shapes=[pltpu.VMEM((2, TOPK * tm, SUB, LANE), F32), pltpu.SemaphoreType.DMA((2,))],
    )
    return pl.pallas_call(
        functools.partial(_combine_kernel, tm=tm, nt=nt, final_norm=final_norm),
        grid_spec=grid_spec,
        out_shape=jax.ShapeDtypeStruct((n_tok, D), F32),
        compiler_params=_cparams(("arbitrary",)),
        name="combine",
    )(plan["cnt"], plan["gstart"], x2, ti, tw, g_final.reshape(1, D), ltri, utri, ybuf)


def _moe_plan(tile_cnt, n_tok):
    cnt = tile_cnt[:, 0, 0:NE].astype(I32)
    before = jnp.cumsum(cnt, axis=0) - cnt
    total = jnp.sum(cnt, axis=0)
    padded = (total + MBLK - 1) // MBLK * MBLK
    p_end = jnp.cumsum(padded)
    p_start = p_end - padded
    n_blk = n_tok * TOPK // MBLK + NE
    blk_start = jnp.arange(n_blk, dtype=I32) * MBLK
    return {
        "cnt": cnt.reshape(-1),
        "gstart": (p_start[None, :] + before).reshape(-1).astype(I32),
        "pad_start": (p_start + total).astype(I32),
        "pad_len": (padded - total).astype(I32),
        "blk_exp": jnp.minimum(jnp.sum(p_end[None, :] <= blk_start[:, None], axis=1), NE - 1).astype(I32),
        "nact": (p_end[-1] // MBLK).astype(I32).reshape(1),
        "n_slots": n_blk * MBLK,
    }


def _layer(x, mem, g_mix, w_in, pe_cmp_k, pe_cmp_v, w_phi1_k, w_phi2_k, w_phi1_v, w_phi2_v,
           w_conv, b_conv, w_q_m, w_k_m, w_v_m, b_i, b_f, mh_gain, skip_m,
           w_up_a, w_up_b, w_o_mix, g_x, g_mem, w_q_x, w_k_x, w_v_x, w_o_x,
           g_ffn, w_router, b_router, w_gu, b_gu, w_down, b_down, g_final, final_norm):
    B, T, _ = x.shape
    kcmp, vcmp, kslc, kwin, tt, gt, gn, xo, gab = _inproj(x, g_mix, w_in, tm=256)
    kc, vct = _compress(kcmp, vcmp, pe_cmp_k, pe_cmp_v, w_phi1_k, w_phi2_k, w_phi1_v, w_phi2_v)
    ya = _nsa(tt, kslc, kwin, kc, vct, gt, tq=WIN)
    yb = _mlstm(xo, gn, gt, w_conv, b_conv, w_q_m, w_k_m, w_v_m, b_i, b_f, mh_gain, skip_m, L=256)
    kx, vx = _memkv(mem, g_mem, w_k_x, w_v_x)
    x2, h3, ti, tw, cnt = _mid(x, ya, yb, gab, w_up_a, w_up_b, w_o_mix, g_x, w_q_x, kx, vx, w_o_x,
                               g_ffn, w_router, b_router, tm=256)
    n_tok = B * T
    plan = _moe_plan(cnt, n_tok)
    ti = ti.reshape(n_tok, LANE)
    xbuf = _dispatch(h3.reshape(n_tok, D), ti, plan, plan["n_slots"], tm=256)
    ybuf = _experts(xbuf, plan, w_gu, b_gu, w_down, b_down)
    out = _combine(plan, x2.reshape(n_tok, D), ti, tw.reshape(n_tok, LANE), g_final, ybuf, tm=256,
                   final_norm=final_norm)
    return out.reshape(B, T, D)


def kernel(x, mem, g_mix, w_in, pe_cmp_k, pe_cmp_v, w_phi1_k, w_phi2_k, w_phi1_v, w_phi2_v, w_conv, b_conv, w_q_m, w_k_m, w_v_m, b_i, b_f, mh_gain, skip_m, w_up_a, w_up_b, w_o_mix, g_x, g_mem, w_q_x, w_k_x, w_v_x, w_o_x, g_ffn, w_router, b_router, w_gu, b_gu, w_down, b_down, g_final):
    layers = (g_mix, w_in, pe_cmp_k, pe_cmp_v, w_phi1_k, w_phi2_k, w_phi1_v, w_phi2_v, w_conv, b_conv,
              w_q_m, w_k_m, w_v_m, b_i, b_f, mh_gain, skip_m, w_up_a, w_up_b, w_o_mix, g_x, g_mem,
              w_q_x, w_k_x, w_v_x, w_o_x, g_ffn, w_router, b_router, w_gu, b_gu, w_down, b_down)
    depth = g_mix.shape[0]
    for l in range(depth):
        x = _layer(x, mem, *(w[l] for w in layers), g_final, final_norm=(l == depth - 1))
    return x
```

```python
import functools

import jax
import jax.numpy as jnp
from jax import lax
from jax.experimental import pallas as pl
from jax.experimental.pallas import tpu as pltpu

F32 = jnp.float32
BF16 = jnp.bfloat16
I32 = jnp.int32

D = 1024
EPS = 1e-5
NEG = -1e30
BIG = 1e30
LOWEST = -3e38

G = 2
HG = 4
DH = 64
CMP_STRIDE = 16
SLC = 64
NSEL = 16
WIN = 512
PHI = 128
NBP = 128
LOG2E = 1.4426950408889634
HB = 4
DHB = 128
MI = HB * DHB
HX = 4
DHX = D // HX
NE = 32
TOPK = 4
DFF = D
LIMIT = 7.0
ALPHA = 1.702
MBLK = 256

LANE = 128
SUB = 8
VMEM_LIMIT = 56 * 1024 * 1024

_O_Q = 0
_O_KCMP = 512
_O_VCMP = 640
_O_KSLC = 768
_O_VSLC = 896
_O_KWIN = 1024
_O_VWIN = 1152
_O_GNSA = 1280
_O_XM = 1304
_O_OPRE = 1816
_O_IPRE = 2328
_O_FPRE = 2332
_O_GA = 2336
_O_GB = 3360


def _cparams(sem):
    return pltpu.CompilerParams(dimension_semantics=sem, vmem_limit_bytes=VMEM_LIMIT)


def _rms(x, g):
    ms = jnp.mean(x * x, axis=-1, keepdims=True)
    return x * lax.rsqrt(ms + EPS) * g


def _dot(a, b):
    return jnp.dot(a, b, preferred_element_type=F32)


def _dot_nt(a, b):
    return lax.dot_general(a, b, (((1,), (1,)), ((), ())), preferred_element_type=F32)


def _split(x):
    hi = x.astype(BF16)
    lo = (x - hi.astype(F32)).astype(BF16)
    return hi, lo


def _inproj_kernel(x_ref, g_ref, wkn_ref, wt_ref, wgt_ref, wgn_ref, wxo_ref, wgab_ref,
                   kcmp_ref, vcmp_ref, kslc_ref, kwin_ref, tt_ref, gt_ref, gn_ref, xo_ref, gab_ref):
    hb = _rms(x_ref[0], g_ref[...]).astype(BF16)
    kn = _dot(hb, wkn_ref[...]).astype(BF16)
    kcmp_ref[0] = kn[:, 0:128]
    vcmp_ref[0] = kn[:, 128:256]
    kslc_ref[0] = kn[:, 256:384]
    kwin_ref[0] = kn[:, 384:512]
    tt_ref[0] = _dot_nt(wt_ref[...], hb).astype(BF16)
    gt_ref[0] = _dot_nt(wgt_ref[...], hb)
    gn_ref[0] = _dot(hb, wgn_ref[...])
    xo_ref[0] = _dot(hb, wxo_ref[...])
    gab_ref[0] = _dot(hb, wgab_ref[...])


def _inproj(x, g_mix, w_in, tm):
    B, T, _ = x.shape
    w = w_in
    wkn = jnp.concatenate([w[:, _O_KCMP:_O_VCMP], w[:, _O_VCMP:_O_KSLC], w[:, _O_KSLC:_O_VSLC],
                           w[:, _O_KWIN:_O_VWIN]], axis=1).astype(BF16)
    wt = jnp.concatenate([w[:, _O_Q:_O_KCMP] * (DH ** -0.5 * LOG2E), w[:, _O_VSLC:_O_KWIN],
                          w[:, _O_VWIN:_O_GNSA]], axis=1).T.astype(BF16)
    wg = jnp.concatenate([w[:, _O_GNSA:_O_XM], w[:, _O_IPRE:_O_GA]], axis=1)
    wgt = wg.T.astype(BF16)
    wgn = jnp.pad(wg, ((0, 0), (0, LANE - 32))).astype(BF16)
    wxo = w[:, _O_XM:_O_IPRE].astype(BF16)
    wgab = w[:, _O_GA:].astype(BF16)
    nt = T // tm
    row = lambda b, i: (b, i, 0)
    col = lambda b, i: (b, 0, i)
    full = lambda b, i: (0, 0)
    out_shape = (
        jax.ShapeDtypeStruct((B, T, 128), BF16), jax.ShapeDtypeStruct((B, T, 128), BF16),
        jax.ShapeDtypeStruct((B, T, 128), BF16), jax.ShapeDtypeStruct((B, T, 128), BF16),
        jax.ShapeDtypeStruct((B, 768, T), BF16), jax.ShapeDtypeStruct((B, 32, T), F32),
        jax.ShapeDtypeStruct((B, T, 128), F32), jax.ShapeDtypeStruct((B, T, 1024), F32),
        jax.ShapeDtypeStruct((B, T, 2048), F32),
    )
    return pl.pallas_call(
        _inproj_kernel,
        grid=(B, nt),
        in_specs=[
            pl.BlockSpec((1, tm, D), row), pl.BlockSpec((1, D), full),
            pl.BlockSpec((D, 512), full), pl.BlockSpec((768, D), full), pl.BlockSpec((32, D), full),
            pl.BlockSpec((D, 128), full), pl.BlockSpec((D, 1024), full), pl.BlockSpec((D, 2048), full),
        ],
        out_specs=(
            pl.BlockSpec((1, tm, 128), row), pl.BlockSpec((1, tm, 128), row),
            pl.BlockSpec((1, tm, 128), row), pl.BlockSpec((1, tm, 128), row),
            pl.BlockSpec((1, 768, tm), col), pl.BlockSpec((1, 32, tm), col),
            pl.BlockSpec((1, tm, 128), row), pl.BlockSpec((1, tm, 1024), row),
            pl.BlockSpec((1, tm, 2048), row),
        ),
        out_shape=out_shape,
        compiler_params=_cparams(("parallel", "parallel")),
        name="inproj",
    )(x, g_mix.reshape(1, D), wkn, wt, wgt, wgn, wxo, wgab)


def _compress_kernel(xk_ref, xv_ref, pek_ref, pev_ref, wklo_ref, wkhi_ref, wk2_ref,
                     wvlo_ref, wvhi_ref, wv2t_ref, kc_ref, vct_ref):
    n_sub = xk_ref.shape[1]

    def hidden(x, pe_ref, wlo_ref, whi_ref):
        wlo = wlo_ref[...]
        whi = whi_ref[...]
        c = _dot(pe_ref[0], wlo) + _dot(pe_ref[1], whi)
        a = _dot(x, wlo)
        b = _dot(x, whi)
        pre = a + pltpu.roll(b, n_sub - 1, 0) + c[0:1, :]
        return jax.nn.gelu(pre).astype(BF16)

    hk = hidden(xk_ref[0], pek_ref, wklo_ref, wkhi_ref)
    kc_ref[0] = _dot(hk, wk2_ref[...]).astype(BF16)
    hv = hidden(xv_ref[0], pev_ref, wvlo_ref, wvhi_ref)
    vct_ref[0] = _dot_nt(wv2t_ref[...], hv).astype(BF16)


def _compress(kcmp, vcmp, pe_k, pe_v, w1k, w2k, w1v, w2v):
    B, T, _ = kcmp.shape
    n_sub = T // CMP_STRIDE
    eye = jnp.eye(G, dtype=F32)

    def big1(w1):
        lo = w1[:CMP_STRIDE * DH].reshape(CMP_STRIDE, DH, PHI)
        hi = w1[CMP_STRIDE * DH:].reshape(CMP_STRIDE, DH, PHI)
        f = lambda a: jnp.einsum('rdp,gh->rgdhp', a, eye).reshape(CMP_STRIDE * G * DH, G * PHI).astype(BF16)
        return f(lo), f(hi)

    def big2(w2):
        return jnp.einsum('pd,gh->gphd', w2, eye).reshape(G * PHI, G * DH)

    def bigpe(pe):
        f = lambda a: jnp.broadcast_to(a[:, None, :], (CMP_STRIDE, G, DH)).reshape(1, CMP_STRIDE * G * DH)
        both = jnp.stack([f(pe[:CMP_STRIDE]), f(pe[CMP_STRIDE:])], axis=0)
        return jnp.broadcast_to(both, (2, SUB, CMP_STRIDE * G * DH)).astype(BF16)

    wklo, wkhi = big1(w1k)
    wvlo, wvhi = big1(w1v)
    wk2 = big2(w2k).astype(BF16)
    wv2t = big2(w2v).T.astype(BF16)
    kw = CMP_STRIDE * G * DH
    xk = kcmp.reshape(B, n_sub, kw)
    xv = vcmp.reshape(B, n_sub, kw)
    c2 = lambda b: (0, 0)
    c3 = lambda b: (0, 0, 0)
    bsel = lambda b: (b, 0, 0)
    return pl.pallas_call(
        _compress_kernel,
        grid=(B,),
        in_specs=[
            pl.BlockSpec((1, n_sub, kw), bsel), pl.BlockSpec((1, n_sub, kw), bsel),
            pl.BlockSpec((2, SUB, kw), c3), pl.BlockSpec((2, SUB, kw), c3),
            pl.BlockSpec((kw, G * PHI), c2), pl.BlockSpec((kw, G * PHI), c2), pl.BlockSpec((G * PHI, G * DH), c2),
            pl.BlockSpec((kw, G * PHI), c2), pl.BlockSpec((kw, G * PHI), c2), pl.BlockSpec((G * DH, G * PHI), c2),
        ],
        out_specs=(pl.BlockSpec((1, n_sub, G * DH), bsel), pl.BlockSpec((1, G * DH, n_sub), bsel)),
        out_shape=(jax.ShapeDtypeStruct((B, n_sub, G * DH), BF16),
                   jax.ShapeDtypeStruct((B, G * DH, n_sub), BF16)),
        compiler_params=_cparams(("parallel",)),
        name="compress",
    )(xk, xv, bigpe(pe_k), bigpe(pe_v), wklo, wkhi, wk2, wvlo, wvhi, wv2t)


def _topk_mask_axis0(score, k):
    n = score.shape[0]
    idx = lax.broadcasted_iota(I32, score.shape, 0)
    work = score
    for _ in range(k):
        m = jnp.max(work, axis=0, keepdims=True)
        first = jnp.min(jnp.where(work == m, idx, n), axis=0, keepdims=True)
        work = jnp.where(idx == first, LOWEST, work)
    return work < 0.5 * LOWEST


def _nsa_kernel(qt_ref, kslc_ref, vslct_ref, kwin_ref, vwint_ref, kc_ref, vct_ref, gt_ref,
                at_ref, e_ref, out_ref, acc_ref, accw_ref, *, tq):
    T = kslc_ref.shape[1]
    n_sub = T // CMP_STRIDE
    nsel = min(NSEL, T // SLC)
    i = pl.program_id(1)
    q0 = i * tq
    t_row = q0 + lax.broadcasted_iota(I32, (1, tq), 1)
    kcv = kc_ref[0]
    cmp_end = lax.broadcasted_iota(I32, (n_sub, 1), 0) * CMP_STRIDE + (2 * CMP_STRIDE - 1)
    bias_c = jnp.where(cmp_end <= t_row, 0.0, NEG)
    col_ok = t_row >= 2 * CMP_STRIDE - 1
    k_loc = lax.broadcasted_iota(I32, (tq, tq), 0)
    q_loc = lax.broadcasted_iota(I32, (tq, tq), 1)
    tri_bias = jnp.where(k_loc <= q_loc, 0.0, NEG)
    prev_bias = jnp.where(k_loc > q_loc + jnp.where(i > 0, 0, tq), 0.0, NEG)
    blk = lax.broadcasted_iota(I32, (NBP, tq), 0)
    cur = t_row // SLC
    at = at_ref[...]
    zeros_half = jnp.zeros((DH, tq), BF16)
    ones_k = jnp.ones((2 * SUB, tq), BF16)
    pieces = []
    for g in range(G):
        glo, ghi = g * DH, (g + 1) * DH
        qps = []
        for h in range(HG):
            r0 = (g * HG + h) * DH
            qh = qt_ref[0, r0:r0 + DH, :]
            qps.append(jnp.concatenate([qh, zeros_half] if g == 0 else [zeros_half, qh], axis=0))
        qp_all = jnp.concatenate(qps, axis=1)

        vc_aug = jnp.concatenate([vct_ref[0, glo:ghi, :], jnp.ones((2 * SUB, n_sub), BF16), at], axis=0)
        s_c = _dot(kcv, qp_all)
        imp = jnp.zeros((NBP, tq), F32)
        o_cmp = []
        for h in range(HG):
            s_h = s_c[:, h * tq:(h + 1) * tq] + bias_c
            p = jnp.exp2(s_h - jnp.max(s_h, axis=0, keepdims=True)).astype(BF16)
            r = _dot(vc_aug, p)
            inv = jnp.where(col_ok, 1.0 / r[DH:DH + 1], 0.0)
            o_cmp.append(r[0:DH] * inv)
            imp = imp + r[DH + 2 * SUB:] * inv
        forced = jnp.where(blk == 0, 0.0, jnp.where(blk == cur, 0.0, jnp.where(blk == cur - 1, 0.0, NEG)))
        free = jnp.where(blk <= cur, jnp.where(forced < 0.0, imp, NEG), NEG)
        picked = _topk_mask_axis0(free, nsel - 3)
        selbias = jnp.where(picked, 0.0, forced).astype(BF16)
        q_aug = jnp.concatenate([qp_all, jnp.concatenate([selbias] * HG, axis=1)], axis=0)
        def chunk(k_ref, vt_ref, acc, rhs, c, ms, bias, glo=glo, ghi=ghi):
            k0 = pl.multiple_of(c * tq, tq)
            lhs = k_ref[0, pl.ds(k0, tq), :]
            if rhs.shape[0] == 2 * LANE:
                lhs = jnp.concatenate([lhs, e_ref[pl.ds(k0, tq), :]], axis=1)
            v_aug = jnp.concatenate([vt_ref[0, glo:ghi, pl.ds(k0, tq)], ones_k], axis=0)
            s = _dot(lhs, rhs)
            new_ms = []
            for h in range(HG):
                s_h = s[:, h * tq:(h + 1) * tq]
                if bias is not None:
                    s_h = s_h + bias
                m_new = jnp.maximum(ms[h], jnp.max(s_h, axis=0, keepdims=True))
                alpha = jnp.exp2(ms[h] - m_new)
                p = jnp.exp2(s_h - m_new).astype(BF16)
                acc[h] = alpha * acc[h] + _dot(v_aug, p)
                new_ms.append(m_new)
            return tuple(new_ms)

        ms0 = tuple(jnp.full((1, tq), NEG, F32) for _ in range(HG))
        acc_ref[...] = jnp.zeros(acc_ref.shape, F32)
        ms = lax.fori_loop(0, i, lambda c, ms, q_aug=q_aug: chunk(kslc_ref, vslct_ref, acc_ref, q_aug, c, ms, None),
                           ms0)
        chunk(kslc_ref, vslct_ref, acc_ref, q_aug, i, ms, tri_bias)
        accw_ref[...] = jnp.zeros(accw_ref.shape, F32)
        ms = chunk(kwin_ref, vwint_ref, accw_ref, qp_all, i, ms0, tri_bias)
        chunk(kwin_ref, vwint_ref, accw_ref, qp_all, jnp.maximum(i - 1, 0), ms, prev_bias)
        for h in range(HG):
            accv = acc_ref[h]
            o_slc = accv[0:DH] * (1.0 / accv[DH:DH + 1])
            accv = accw_ref[h]
            o_win = accv[0:DH] * (1.0 / accv[DH:DH + 1])
            gi = (g * HG + h) * 3
            sg = jax.nn.sigmoid(gt_ref[0, gi:gi + 3, :])
            pieces.append(sg[0:1] * o_cmp[h] + sg[1:2] * o_slc + sg[2:3] * o_win)
    yt = jnp.concatenate(pieces, axis=0)
    out_ref[0] = yt.T.astype(BF16)


def _nsa(tt, kslc, kwin, kc, vct, gt, tq):
    B, T, _ = kslc.shape
    n_sub = T // CMP_STRIDE
    nb = T // SLC
    assert nb <= NBP and T % tq == 0
    assert tq == WIN
    cidx = jnp.arange(n_sub)[None, :]
    bidx = jnp.arange(NBP)[:, None]
    off = cidx - (SLC // CMP_STRIDE) * bidx
    at = (jnp.where((off >= -1) & (off <= 3), 1.0, 0.0)
          + jnp.where((off >= 0) & (off <= 2), 1.0, 0.0))
    at = jnp.where((cidx < n_sub - 1) & (bidx < nb), at, 0.0).astype(BF16)
    e = (jnp.arange(T)[:, None] // SLC == jnp.arange(NBP)[None, :]).astype(BF16)
    nt = T // tq
    kern = functools.partial(_nsa_kernel, tq=tq)
    bfull = lambda b, i: (b, 0, 0)
    return pl.pallas_call(
        kern,
        grid=(B, nt),
        in_specs=[
            pl.BlockSpec((1, 512, tq), lambda b, i: (b, 0, i)),
            pl.BlockSpec((1, T, 128), bfull),
            pl.BlockSpec((1, 128, T), lambda b, i: (b, 4, 0)),
            pl.BlockSpec((1, T, 128), bfull),
            pl.BlockSpec((1, 128, T), lambda b, i: (b, 5, 0)),
            pl.BlockSpec((1, n_sub, 128), bfull),
            pl.BlockSpec((1, 128, n_sub), bfull),
            pl.BlockSpec((1, 32, tq), lambda b, i: (b, 0, i)),
            pl.BlockSpec((NBP, n_sub), lambda b, i: (0, 0)),
            pl.BlockSpec((T, NBP), lambda b, i: (0, 0)),
        ],
        out_specs=pl.BlockSpec((1, tq, 512), lambda b, i: (b, i, 0)),
        out_shape=jax.ShapeDtypeStruct((B, T, 512), BF16),
        scratch_shapes=[pltpu.VMEM((HG, DH + 2 * SUB, tq), F32), pltpu.VMEM((HG, DH + 2 * SUB, tq), F32)],
        compiler_params=_cparams(("parallel", "parallel")),
        name="nsa",
    )(tt, kslc, tt, kwin, tt, kc, vct, gt, at, e)


def _log_sigmoid(x):
    return jnp.minimum(x, 0.0) - jnp.log(1.0 + jnp.exp(-jnp.abs(x)))


def _mlstm_kernel(xo_ref, gn_ref, gt_ref, bn_ref, bt_ref, wconv_ref, bconv_ref, wq_ref, wk_ref, wv_ref,
                  gain_ref, skip_ref, tri_ref, trit_ref, out_ref, xbuf, caug, mst, *, L):
    c = pl.program_id(1)

    @pl.when(c == 0)
    def _():
        xbuf[...] = jnp.zeros(xbuf.shape, F32)
        caug[...] = jnp.zeros(caug.shape, F32)
        mst[...] = jnp.zeros(mst.shape, F32)

    xbuf[0:SUB, :] = xbuf[L:L + SUB, :]
    xbuf[SUB:SUB + L, :] = xo_ref[0, :, 0:MI]
    conv = bconv_ref[...] + wconv_ref[3:4, :] * xbuf[SUB:SUB + L, :]
    for r in range(3):
        s = 3 - r
        conv = conv + wconv_ref[r:r + 1, :] * xbuf[SUB - s:SUB - s + L, :]
    xc = conv * jax.nn.sigmoid(conv)

    gn = gn_ref[0] + bn_ref[...]
    gt = gt_ref[0] + bt_ref[...]
    lf_hi, lf_lo = _split(_log_sigmoid(gn))
    tri = tri_ref[...]
    b_col_all = _dot(tri, lf_hi) + _dot(tri, lf_lo)
    lft_hi, lft_lo = _split(_log_sigmoid(gt))
    trit = trit_ref[...]
    b_row_all = _dot(lft_hi, trit) + _dot(lft_lo, trit)
    ti = lax.broadcasted_iota(I32, (L, L), 0)
    si = lax.broadcasted_iota(I32, (L, L), 1)
    causal = si <= ti
    one_col = jnp.where(lax.broadcasted_iota(I32, (L, DHB), 1) == 0, 1.0, 0.0).astype(BF16)

    outs = []
    for h in range(HB):
        lo, hi = h * DHB, (h + 1) * DHB
        xch = xc[:, lo:hi]
        xcb = xch.astype(BF16)
        q = _dot(xcb, wq_ref[h]).astype(BF16)
        k = _dot(xcb, wk_ref[h]) * (DHB ** -0.5)
        v = _dot(xo_ref[0, :, lo:hi].astype(BF16), wv_ref[h]).astype(BF16)
        v_aug = jnp.concatenate([v, one_col], axis=1)
        bc = b_col_all[:, 28 + h:29 + h]
        lic = gn[:, 24 + h:25 + h]
        br = b_row_all[28 + h:29 + h, :]
        lir = gt[24 + h:25 + h, :]
        m_prev = mst[h, 0:1, 0:1]
        log_d = jnp.where(causal, bc - br + lir, NEG)
        m_inter = bc + m_prev
        m_t = jnp.maximum(jnp.max(log_d, axis=1, keepdims=True), m_inter)
        dmat = jnp.exp(log_d - m_t)
        s = (_dot_nt(q, k.astype(BF16)) * dmat).astype(BF16)
        inter = jnp.exp(m_inter - m_t)
        c_old = caug[h]
        r = _dot(s, v_aug) + inter * _dot(q, c_old.astype(BF16))
        num = r[:, 0:DHB]
        den = r[:, DHB:DHB + 1]
        hval = num / jnp.maximum(jnp.abs(den), jnp.exp(-m_t))
        b_last = bc[L - 1:L, :]
        log_w = b_last - bc + lic
        m_new = jnp.maximum(b_last + m_prev, jnp.max(log_w, axis=0, keepdims=True))
        w = jnp.exp(log_w - m_new)
        decay = jnp.exp(b_last + m_prev - m_new)
        kwt = (k * w).T.astype(BF16)
        caug[h] = decay * c_old + _dot(kwt, v_aug)
        mst[h] = jnp.broadcast_to(m_new, (SUB, LANE))
        o = jax.nn.sigmoid(xo_ref[0, :, MI + lo:MI + hi]) * hval
        mu = jnp.mean(o, axis=-1, keepdims=True)
        var = jnp.mean((o - mu) ** 2, axis=-1, keepdims=True)
        hn = (o - mu) * lax.rsqrt(var + EPS) * gain_ref[:, lo:hi]
        outs.append(hn + skip_ref[:, lo:hi] * xch)
    out_ref[0] = jnp.concatenate(outs, axis=1).astype(BF16)


def _mlstm(xo, gn, gt, w_conv, b_conv, w_q, w_k, w_v, b_i, b_f, mh_gain, skip, L):
    B, T, _ = xo.shape
    bias = jnp.concatenate([jnp.zeros((24,), F32), b_i, b_f])
    bn = jnp.pad(bias, (0, LANE - 32)).reshape(1, LANE)
    bt = bias.reshape(32, 1)
    tri = (jnp.arange(L)[None, :] <= jnp.arange(L)[:, None]).astype(BF16)
    c2 = lambda b, c: (0, 0)
    c3 = lambda b, c: (0, 0, 0)
    return pl.pallas_call(
        functools.partial(_mlstm_kernel, L=L),
        grid=(B, T // L),
        in_specs=[
            pl.BlockSpec((1, L, 2 * MI), lambda b, c: (b, c, 0)),
            pl.BlockSpec((1, L, LANE), lambda b, c: (b, c, 0)),
            pl.BlockSpec((1, 32, L), lambda b, c: (b, 0, c)),
            pl.BlockSpec((1, LANE), c2), pl.BlockSpec((32, 1), c2),
            pl.BlockSpec((4, MI), c2), pl.BlockSpec((1, MI), c2),
            pl.BlockSpec((HB, DHB, DHB), c3), pl.BlockSpec((HB, DHB, DHB), c3), pl.BlockSpec((HB, DHB, DHB), c3),
            pl.BlockSpec((1, MI), c2), pl.BlockSpec((1, MI), c2),
            pl.BlockSpec((L, L), c2), pl.BlockSpec((L, L), c2),
        ],
        out_specs=pl.BlockSpec((1, L, MI), lambda b, c: (b, c, 0)),
        out_shape=jax.ShapeDtypeStruct((B, T, MI), BF16),
        scratch_shapes=[pltpu.VMEM((L + 2 * SUB, MI), F32), pltpu.VMEM((HB, DHB, 2 * DHB), F32),
                        pltpu.VMEM((HB, SUB, LANE), F32)],
        compiler_params=_cparams(("parallel", "arbitrary")),
        name="mlstm",
    )(xo, gn, gt, bn, bt, w_conv, b_conv.reshape(1, MI), w_q.astype(BF16), w_k.astype(BF16),
      w_v.astype(BF16), mh_gain.reshape(1, MI), skip.reshape(1, MI), tri, tri.T)


def _memkv_kernel(m_ref, g_ref, wk_ref, wv_ref, k_ref, v_ref):
    hb = _rms(m_ref[0], g_ref[...]).astype(BF16)
    k_ref[0] = _dot(hb, wk_ref[...]).astype(BF16)
    v_ref[0] = _dot(hb, wv_ref[...]).astype(BF16)


def _memkv(mem, g_mem, w_k, w_v):
    B, M, _ = mem.shape
    c2 = lambda b: (0, 0)
    bs = lambda b: (b, 0, 0)
    return pl.pallas_call(
        _memkv_kernel,
        grid=(B,),
        in_specs=[pl.BlockSpec((1, M, D), bs), pl.BlockSpec((1, D), c2),
                  pl.BlockSpec((D, D), c2), pl.BlockSpec((D, D), c2)],
        out_specs=(pl.BlockSpec((1, M, D), bs), pl.BlockSpec((1, M, D), bs)),
        out_shape=(jax.ShapeDtypeStruct((B, M, D), BF16), jax.ShapeDtypeStruct((B, M, D), BF16)),
        compiler_params=_cparams(("parallel",)),
        name="memkv",
    )(mem, g_mem.reshape(1, D), w_k.astype(BF16), w_v.astype(BF16))


def _mid_kernel(x_ref, ya_ref, yb_ref, gab_ref, wua_ref, wub_ref, wo_ref, gx_ref, wq_ref, kx_ref, vx_ref,
                wox_ref, gf_ref, wrh_ref, wrl_ref, br_ref,
                x2_ref, h3_ref, ti_ref, tw_ref, cnt_ref, *, tm):
    a = _dot(ya_ref[0], wua_ref[...])
    b = _dot(yb_ref[0], wub_ref[...])
    merged = jax.nn.sigmoid(gab_ref[0, :, 0:D]) * a + jax.nn.sigmoid(gab_ref[0, :, D:2 * D]) * b
    x1 = x_ref[0] + _dot(merged.astype(BF16), wo_ref[...])
    q = (_dot(_rms(x1, gx_ref[...]).astype(BF16), wq_ref[...]) * (DHX ** -0.5)).astype(BF16)
    kx = kx_ref[0]
    vx = vx_ref[0]
    heads = []
    for h in range(HX):
        lo, hi = h * DHX, (h + 1) * DHX
        s = _dot_nt(q[:, lo:hi], kx[:, lo:hi])
        m = jnp.max(s, axis=-1, keepdims=True)
        e = jnp.exp(s - m)
        p = e * (1.0 / jnp.sum(e, axis=-1, keepdims=True))
        heads.append(_dot(p.astype(BF16), vx[:, lo:hi]).astype(BF16))
    x2 = x1 + _dot(jnp.concatenate(heads, axis=1), wox_ref[...])
    x2_ref[0] = x2
    h3 = _rms(x2, gf_ref[...])
    h3_ref[0] = h3
    h_hi, h_lo = _split(h3)
    logits = (_dot(h_hi, wrh_ref[...]) + _dot(h_lo, wrh_ref[...]) + _dot(h_hi, wrl_ref[...])) + br_ref[...]
    lane = lax.broadcasted_iota(I32, (tm, LANE), 1)
    work = logits
    vals, idxs = [], []
    multi = jnp.zeros((tm, LANE), F32)
    for _ in range(TOPK):
        m = jnp.max(work, axis=-1, keepdims=True)
        first = jnp.min(jnp.where(work == m, lane, LANE), axis=-1, keepdims=True)
        pick = lane == first
        vals.append(m)
        idxs.append(first)
        multi = jnp.where(pick, 1.0, multi)
        work = jnp.where(pick, LOWEST, work)
    es = [jnp.exp(v - vals[0]) for v in vals]
    inv = 1.0 / (es[0] + es[1] + es[2] + es[3])
    tw = jnp.zeros((tm, LANE), F32)
    for k in range(TOPK):
        tw = jnp.where(lane == k, es[k] * inv, tw)
    tw_ref[0] = tw
    ti = jnp.zeros((tm, LANE), I32)
    for k in range(TOPK):
        ti = jnp.where(lane == k, idxs[k], ti)
    ti_ref[0] = ti
    cnt_ref[0] = jnp.broadcast_to(jnp.sum(multi, axis=0, keepdims=True), (SUB, LANE))


def _mid(x, ya, yb, gab, w_up_a, w_up_b, w_o_mix, g_x, w_q_x, kx, vx, w_o_x, g_ffn, w_router, b_router, tm):
    B, T, _ = x.shape
    M = kx.shape[1]
    wr = jnp.pad(w_router, ((0, 0), (0, LANE - NE)))
    wrh, wrl = _split(wr)
    br = jnp.concatenate([b_router, jnp.full((LANE - NE,), NEG, F32)]).reshape(1, LANE)
    c2 = lambda b, i: (0, 0)
    row = lambda b, i: (b, i, 0)
    bs = lambda b, i: (b, 0, 0)
    nt = T // tm
    return pl.pallas_call(
        functools.partial(_mid_kernel, tm=tm),
        grid=(B, nt),
        in_specs=[
            pl.BlockSpec((1, tm, D), row), pl.BlockSpec((1, tm, 512), row), pl.BlockSpec((1, tm, MI), row),
            pl.BlockSpec((1, tm, 2 * D), row),
            pl.BlockSpec((512, D), c2), pl.BlockSpec((MI, D), c2), pl.BlockSpec((D, D), c2),
            pl.BlockSpec((1, D), c2), pl.BlockSpec((D, D), c2),
            pl.BlockSpec((1, M, D), bs), pl.BlockSpec((1, M, D), bs),
            pl.BlockSpec((D, D), c2), pl.BlockSpec((1, D), c2),
            pl.BlockSpec((D, LANE), c2), pl.BlockSpec((D, LANE), c2), pl.BlockSpec((1, LANE), c2),
        ],
        out_specs=(
            pl.BlockSpec((1, tm, D), row), pl.BlockSpec((1, tm, D), row),
            pl.BlockSpec((1, tm, LANE), row), pl.BlockSpec((1, tm, LANE), row),
            pl.BlockSpec((1, SUB, LANE), lambda b, i: (b * nt + i, 0, 0)),
        ),
        out_shape=(
            jax.ShapeDtypeStruct((B, T, D), F32), jax.ShapeDtypeStruct((B, T, D), F32),
            jax.ShapeDtypeStruct((B, T, LANE), I32), jax.ShapeDtypeStruct((B, T, LANE), F32),
            jax.ShapeDtypeStruct((B * nt, SUB, LANE), F32),
        ),
        compiler_params=_cparams(("parallel", "parallel")),
        name="mid",
    )(x, ya, yb, gab, w_up_a.astype(BF16), w_up_b.astype(BF16), w_o_mix.astype(BF16), g_x.reshape(1, D),
      w_q_x.astype(BF16), kx, vx, w_o_x.astype(BF16), g_ffn.reshape(1, D), wrh, wrl, br)


def _local_capacity(tm):
    return TOPK * tm + NE * SUB


def _local_rows(ti, ltri, utri):
    tm = ti.shape[0]
    lane = lax.broadcasted_iota(I32, (tm, LANE), 1)
    hots = [lane == ti[:, k:k + 1] for k in range(TOPK)]
    multi = jnp.zeros((tm, LANE), F32)
    for hot in hots:
        multi = jnp.where(hot, 1.0, multi)
    prior = _dot(ltri, multi.astype(BF16))
    cnt = jnp.broadcast_to(jnp.sum(multi, axis=0, keepdims=True), (SUB, LANE))
    cnt = jnp.ceil(cnt * (1.0 / SUB)) * SUB
    loc = _dot(cnt.astype(BF16), utri)[0:1]
    base = prior + loc
    return [jnp.sum(jnp.where(hot, base, 0.0), axis=-1, keepdims=True) for hot in hots]


def _run_copies(cnt_ref, gs_ref, tile, make_copy):
    def body(e, local):
        n = pl.multiple_of(cnt_ref[tile * NE + e], SUB)

        @pl.when(n > 0)
        def _():
            make_copy(pl.multiple_of(local, SUB), pl.multiple_of(gs_ref[tile * NE + e], SUB), n).start()

        return local + n

    lax.fori_loop(0, NE, body, 0)


def _dispatch_kernel(cnt_ref, gs_ref, rows_ref, ps_ref, pn_ref, na_ref, h3_ref, ti_ref, ltri_ref, utri_ref,
                     xbuf_hbm, xl, zbuf, sem, zsem, *, tm, nt, n_blk):
    j = pl.program_id(0)
    slot = j % 2
    rows = xl.shape[1]

    def wait_tile(s, tile):
        n = pl.multiple_of(rows_ref[tile], SUB)
        pltpu.make_async_copy(xl.at[s, pl.ds(0, n), :], xbuf_hbm.at[pl.ds(0, n), :], sem.at[s]).wait()

    def pad_copy(e):
        n = pl.multiple_of(pn_ref[e], SUB)
        return pltpu.make_async_copy(zbuf.at[pl.ds(0, n), :],
                                     xbuf_hbm.at[pl.ds(pl.multiple_of(ps_ref[e], SUB), n), :], zsem)

    def tail_copy(b):
        return pltpu.make_async_copy(zbuf, xbuf_hbm.at[pl.ds(pl.multiple_of(b * MBLK, MBLK), MBLK), :], zsem)

    def for_tail_blocks(fn):
        def body(b, carry):
            fn(b)
            return carry

        lax.fori_loop(na_ref[0], n_blk, body, 0)

    @pl.when(j >= 2)
    def _():
        wait_tile(slot, j - 2)

    @pl.when(j == 0)
    def _():
        zbuf[...] = jnp.zeros(zbuf.shape, F32)
        for e in range(NE):
            @pl.when(pn_ref[e] > 0)
            def _(e=e):
                pad_copy(e).start()
        for_tail_blocks(lambda b: tail_copy(b).start())

    lrs = _local_rows(ti_ref[...], ltri_ref[...], utri_ref[...])
    lane = lax.broadcasted_iota(I32, (tm, LANE), 1)
    lmat = jnp.zeros((tm, LANE), F32)
    for k in range(TOPK):
        lmat = jnp.where(lane == k, lrs[k], lmat)
    lt = lmat.T
    rio = lax.broadcasted_iota(I32, (rows, 1), 0).astype(F32)
    st = jnp.zeros((rows, tm), F32)
    for k in range(TOPK):
        st = jnp.where(rio == lt[k:k + 1, :], 1.0, st)
    xl[slot] = _dot(st.astype(BF16), h3_ref[...].astype(BF16))
    _run_copies(cnt_ref, gs_ref, j,
                lambda l, g, n: pltpu.make_async_copy(xl.at[slot, pl.ds(l, n), :], xbuf_hbm.at[pl.ds(g, n), :],
                                                      sem.at[slot]))

    @pl.when(j == nt - 1)
    def _():
        wait_tile(slot, j)
        if nt >= 2:
            wait_tile(1 - slot, j - 1)
        for e in range(NE):
            @pl.when(pn_ref[e] > 0)
            def _(e=e):
                pad_copy(e).wait()
        for_tail_blocks(lambda b: tail_copy(b).wait())


def _dispatch(h3, ti, plan, n_slots, tm):
    n_tok = h3.shape[0]
    nt = n_tok // tm
    ltri = (jnp.arange(tm)[None, :] < jnp.arange(tm)[:, None]).astype(BF16)
    utri = (jnp.arange(LANE)[:, None] < jnp.arange(LANE)[None, :]).astype(BF16)
    c2 = lambda j, *_: (0, 0)
    grid_spec = pltpu.PrefetchScalarGridSpec(
        num_scalar_prefetch=6,
        grid=(nt,),
        in_specs=[
            pl.BlockSpec((tm, D), lambda j, *_: (j, 0)), pl.BlockSpec((tm, LANE), lambda j, *_: (j, 0)),
            pl.BlockSpec((tm, tm), c2), pl.BlockSpec((LANE, LANE), c2),
        ],
        out_specs=pl.BlockSpec(memory_space=pl.ANY),
        scratch_shapes=[pltpu.VMEM((2, _local_capacity(tm), D), F32), pltpu.VMEM((MBLK, D), F32),
                        pltpu.SemaphoreType.DMA((2,)), pltpu.SemaphoreType.DMA(())],
    )
    return pl.pallas_call(
        functools.partial(_dispatch_kernel, tm=tm, nt=nt, n_blk=n_slots // MBLK),
        grid_spec=grid_spec,
        out_shape=jax.ShapeDtypeStruct((n_slots, D), F32),
        compiler_params=_cparams(("arbitrary",)),
        name="dispatch",
    )(plan["cnt"], plan["gstart"], plan["rows"], plan["pad_start"], plan["pad_len"], plan["nact"],
      h3, ti, ltri, utri)


def _expert_kernel(be_ref, na_ref, x_ref, wgu_ref, bgu_ref, wdn_ref, bdn_ref, out_ref, wgu_bf, wdn_bf):
    i = pl.program_id(0)
    nact = na_ref[0]

    @pl.when(i < nact)
    def _():
        @pl.when(jnp.logical_or(i == 0, be_ref[i] != be_ref[jnp.maximum(i - 1, 0)]))
        def _():
            wgu_bf[...] = wgu_ref[0].astype(BF16)
            wdn_bf[...] = wdn_ref[0].astype(BF16)

        gu = _dot(x_ref[...].astype(BF16), wgu_bf[...]) + bgu_ref[0]
        g = jnp.minimum(gu[:, 0:DFF], LIMIT)
        u = jnp.clip(gu[:, DFF:2 * DFF], -LIMIT, LIMIT)
        act = (u + 1.0) * (g * jax.nn.sigmoid(ALPHA * g))
        out_ref[...] = _dot(act.astype(BF16), wdn_bf[...]) + bdn_ref[0]

    @pl.when(i >= nact)
    def _():
        out_ref[...] = jnp.zeros(out_ref.shape, F32)


def _experts(xbuf, plan, w_gu, b_gu, w_down, b_down):
    n_blk = plan["blk_exp"].shape[0]
    wsel = lambda i, be, na: (be[i], 0, 0)
    grid_spec = pltpu.PrefetchScalarGridSpec(
        num_scalar_prefetch=2,
        grid=(n_blk,),
        in_specs=[
            pl.BlockSpec((MBLK, D), lambda i, be, na: (jnp.minimum(i, na[0] - 1), 0)),
            pl.BlockSpec((1, D, 2 * DFF), wsel), pl.BlockSpec((1, 1, 2 * DFF), wsel),
            pl.BlockSpec((1, DFF, D), wsel), pl.BlockSpec((1, 1, D), wsel),
        ],
        out_specs=pl.BlockSpec((MBLK, D), lambda i, be, na: (i, 0)),
        scratch_shapes=[pltpu.VMEM((D, 2 * DFF), BF16), pltpu.VMEM((DFF, D), BF16)],
    )
    return pl.pallas_call(
        _expert_kernel,
        grid_spec=grid_spec,
        out_shape=jax.ShapeDtypeStruct((n_blk * MBLK, D), F32),
        compiler_params=_cparams(("arbitrary",)),
        name="experts",
    )(plan["blk_exp"], plan["nact"], xbuf, w_gu, b_gu.reshape(NE, 1, 2 * DFF), w_down, b_down.reshape(NE, 1, D))


def _combine_kernel(cnt_ref, gs_ref, rows_ref, x2_ref, ti_ref, tw_ref, gfin_ref, ltri_ref, utri_ref, y_hbm,
                    out_ref, yl, sem, *, tm, nt, final_norm):
    j = pl.program_id(0)
    slot = j % 2
    rows = yl.shape[1]

    def issue(tile, s):
        _run_copies(cnt_ref, gs_ref, tile,
                    lambda l, g, n: pltpu.make_async_copy(y_hbm.at[pl.ds(g, n), :], yl.at[s, pl.ds(l, n), :],
                                                          sem.at[s]))

    @pl.when(j == 0)
    def _():
        yl[...] = jnp.zeros(yl.shape, F32)
        issue(0, 0)

    n_rows = pl.multiple_of(rows_ref[j], SUB)
    pltpu.make_async_copy(y_hbm.at[pl.ds(0, n_rows), :], yl.at[slot, pl.ds(0, n_rows), :], sem.at[slot]).wait()

    @pl.when(j + 1 < nt)
    def _():
        issue(j + 1, 1 - slot)

    lrs = _local_rows(ti_ref[...], ltri_ref[...], utri_ref[...])
    rio = lax.broadcasted_iota(I32, (1, rows), 1).astype(F32)
    sw = jnp.zeros((tm, rows), F32)
    for k in range(TOPK):
        sw = jnp.where(rio == lrs[k], tw_ref[:, k:k + 1], sw)
    yv = yl[slot].astype(BF16)
    s_hi, s_lo = _split(sw)
    acc = x2_ref[...] + (_dot(s_hi, yv) + _dot(s_lo, yv))
    if final_norm:
        acc = _rms(acc, gfin_ref[...])
    out_ref[...] = acc


def _combine(plan, x2, ti, tw, g_final, ybuf, tm, final_norm):
    n_tok = x2.shape[0]
    nt = n_tok // tm
    ltri = (jnp.arange(tm)[None, :] < jnp.arange(tm)[:, None]).astype(BF16)
    utri = (jnp.arange(LANE)[:, None] < jnp.arange(LANE)[None, :]).astype(BF16)
    c2 = lambda j, *_: (0, 0)
    tile = lambda j, *_: (j, 0)
    grid_spec = pltpu.PrefetchScalarGridSpec(
        num_scalar_prefetch=3,
        grid=(nt,),
        in_specs=[
            pl.BlockSpec((tm, D), tile), pl.BlockSpec((tm, LANE), tile), pl.BlockSpec((tm, LANE), tile),
            pl.BlockSpec((1, D), c2), pl.BlockSpec((tm, tm), c2), pl.BlockSpec((LANE, LANE), c2),
            pl.BlockSpec(memory_space=pl.ANY),
        ],
        out_specs=pl.BlockSpec((tm, D), tile),
        scratch_shapes=[pltpu.VMEM((2, _local_capacity(tm), D), F32), pltpu.SemaphoreType.DMA((2,))],
    )
    return pl.pallas_call(
        functools.partial(_combine_kernel, tm=tm, nt=nt, final_norm=final_norm),
        grid_spec=grid_spec,
        out_shape=jax.ShapeDtypeStruct((n_tok, D), F32),
        compiler_params=_cparams(("arbitrary",)),
        name="combine",
    )(plan["cnt"], plan["gstart"], plan["rows"], x2, ti, tw, g_final.reshape(1, D), ltri, utri, ybuf)


def _moe_plan(tile_cnt, n_tok):
    n_tiles = tile_cnt.shape[0]
    cnt = (tile_cnt[:, 0, 0:NE].astype(I32) + SUB - 1) // SUB * SUB
    before = jnp.cumsum(cnt, axis=0) - cnt
    total = jnp.sum(cnt, axis=0)
    padded = (total + MBLK - 1) // MBLK * MBLK
    p_end = jnp.cumsum(padded)
    p_start = p_end - padded
    n_blk = (n_tok * TOPK + n_tiles * NE * (SUB - 1) + MBLK - 1) // MBLK + NE
    blk_start = jnp.arange(n_blk, dtype=I32) * MBLK
    return {
        "cnt": cnt.reshape(-1),
        "rows": jnp.sum(cnt, axis=1).astype(I32),
        "gstart": (p_start[None, :] + before).reshape(-1).astype(I32),
        "pad_start": (p_start + total).astype(I32),
        "pad_len": (padded - total).astype(I32),
        "blk_exp": jnp.minimum(jnp.sum(p_end[None, :] <= blk_start[:, None], axis=1), NE - 1).astype(I32),
        "nact": (p_end[-1] // MBLK).astype(I32).reshape(1),
        "n_slots": n_blk * MBLK,
    }


def _layer(x, mem, g_mix, w_in, pe_cmp_k, pe_cmp_v, w_phi1_k, w_phi2_k, w_phi1_v, w_phi2_v,
           w_conv, b_conv, w_q_m, w_k_m, w_v_m, b_i, b_f, mh_gain, skip_m,
           w_up_a, w_up_b, w_o_mix, g_x, g_mem, w_q_x, w_k_x, w_v_x, w_o_x,
           g_ffn, w_router, b_router, w_gu, b_gu, w_down, b_down, g_final, final_norm):
    B, T, _ = x.shape
    kcmp, vcmp, kslc, kwin, tt, gt, gn, xo, gab = _inproj(x, g_mix, w_in, tm=256)
    kc, vct = _compress(kcmp, vcmp, pe_cmp_k, pe_cmp_v, w_phi1_k, w_phi2_k, w_phi1_v, w_phi2_v)
    ya = _nsa(tt, kslc, kwin, kc, vct, gt, tq=WIN)
    yb = _mlstm(xo, gn, gt, w_conv, b_conv, w_q_m, w_k_m, w_v_m, b_i, b_f, mh_gain, skip_m, L=256)
    kx, vx = _memkv(mem, g_mem, w_k_x, w_v_x)
    x2, h3, ti, tw, cnt = _mid(x, ya, yb, gab, w_up_a, w_up_b, w_o_mix, g_x, w_q_x, kx, vx, w_o_x,
                               g_ffn, w_router, b_router, tm=256)
    n_tok = B * T
    plan = _moe_plan(cnt, n_tok)
    ti = ti.reshape(n_tok, LANE)
    xbuf = _dispatch(h3.reshape(n_tok, D), ti, plan, plan["n_slots"], tm=256)
    ybuf = _experts(xbuf, plan, w_gu, b_gu, w_down, b_down)
    out = _combine(plan, x2.reshape(n_tok, D), ti, tw.reshape(n_tok, LANE), g_final, ybuf, tm=256,
                   final_norm=final_norm)
    return out.reshape(B, T, D)


def kernel(x, mem, g_mix, w_in, pe_cmp_k, pe_cmp_v, w_phi1_k, w_phi2_k, w_phi1_v, w_phi2_v, w_conv, b_conv, w_q_m, w_k_m, w_v_m, b_i, b_f, mh_gain, skip_m, w_up_a, w_up_b, w_o_mix, g_x, g_mem, w_q_x, w_k_x, w_v_x, w_o_x, g_ffn, w_router, b_router, w_gu, b_gu, w_down, b_down, g_final):
    layers = (g_mix, w_in, pe_cmp_k, pe_cmp_v, w_phi1_k, w_phi2_k, w_phi1_v, w_phi2_v, w_conv, b_conv,
              w_q_m, w_k_m, w_v_m, b_i, b_f, mh_gain, skip_m, w_up_a, w_up_b, w_o_mix, g_x, g_mem,
              w_q_x, w_k_x, w_v_x, w_o_x, g_ffn, w_router, b_router, w_gu, b_gu, w_down, b_down)
    depth = g_mix.shape[0]
    for l in range(depth):
        x = _layer(x, mem, *(w[l] for w in layers), g_final, final_norm=(l == depth - 1))
    return x
```

```python
import functools

import jax
import jax.numpy as jnp
from jax import lax
from jax.experimental import pallas as pl
from jax.experimental.pallas import tpu as pltpu

F32 = jnp.float32
BF16 = jnp.bfloat16
I32 = jnp.int32

D = 1024
EPS = 1e-5
NEG = -1e30
BIG = 1e30
LOWEST = -3e38

G = 2
HG = 4
DH = 64
CMP_STRIDE = 16
SLC = 64
NSEL = 16
WIN = 512
PHI = 128
NBP = 128
LOG2E = 1.4426950408889634
HB = 4
DHB = 128
MI = HB * DHB
HX = 4
DHX = D // HX
NE = 32
TOPK = 4
DFF = D
LIMIT = 7.0
ALPHA = 1.702
MBLK = 256

LANE = 128
SUB = 8
VMEM_LIMIT = 56 * 1024 * 1024

_O_Q = 0
_O_KCMP = 512
_O_VCMP = 640
_O_KSLC = 768
_O_VSLC = 896
_O_KWIN = 1024
_O_VWIN = 1152
_O_GNSA = 1280
_O_XM = 1304
_O_OPRE = 1816
_O_IPRE = 2328
_O_FPRE = 2332
_O_GA = 2336
_O_GB = 3360


def _cparams(sem):
    return pltpu.CompilerParams(dimension_semantics=sem, vmem_limit_bytes=VMEM_LIMIT)


def _rms(x, g):
    ms = jnp.mean(x * x, axis=-1, keepdims=True)
    return x * lax.rsqrt(ms + EPS) * g


def _dot(a, b):
    return jnp.dot(a, b, preferred_element_type=F32)


def _dot_nt(a, b):
    return lax.dot_general(a, b, (((1,), (1,)), ((), ())), preferred_element_type=F32)


def _split(x):
    hi = x.astype(BF16)
    lo = (x - hi.astype(F32)).astype(BF16)
    return hi, lo


def _inproj_kernel(x_ref, g_ref, wkn_ref, wt_ref, wgt_ref, wgn_ref, wxo_ref, wgab_ref,
                   kcmp_ref, vcmp_ref, kslc_ref, kwin_ref, tt_ref, gt_ref, gn_ref, xo_ref, gab_ref):
    hb = _rms(x_ref[0], g_ref[...]).astype(BF16)
    kn = _dot(hb, wkn_ref[...]).astype(BF16)
    kcmp_ref[0] = kn[:, 0:128]
    vcmp_ref[0] = kn[:, 128:256]
    kslc_ref[0] = kn[:, 256:384]
    kwin_ref[0] = kn[:, 384:512]
    tt_ref[0] = _dot_nt(wt_ref[...], hb).astype(BF16)
    gt_ref[0] = _dot_nt(wgt_ref[...], hb)
    gn_ref[0] = _dot(hb, wgn_ref[...])
    xo_ref[0] = _dot(hb, wxo_ref[...])
    gab_ref[0] = _dot(hb, wgab_ref[...])


def _inproj(x, g_mix, w_in, tm):
    B, T, _ = x.shape
    w = w_in
    wkn = jnp.concatenate([w[:, _O_KCMP:_O_VCMP], w[:, _O_VCMP:_O_KSLC], w[:, _O_KSLC:_O_VSLC],
                           w[:, _O_KWIN:_O_VWIN]], axis=1).astype(BF16)
    wt = jnp.concatenate([w[:, _O_Q:_O_KCMP] * (DH ** -0.5 * LOG2E), w[:, _O_VSLC:_O_KWIN],
                          w[:, _O_VWIN:_O_GNSA]], axis=1).T.astype(BF16)
    wg = jnp.concatenate([w[:, _O_GNSA:_O_XM], w[:, _O_IPRE:_O_GA]], axis=1)
    wgt = wg.T.astype(BF16)
    wgn = jnp.pad(wg, ((0, 0), (0, LANE - 32))).astype(BF16)
    wxo = w[:, _O_XM:_O_IPRE].astype(BF16)
    wgab = w[:, _O_GA:].astype(BF16)
    nt = T // tm
    row = lambda b, i: (b, i, 0)
    col = lambda b, i: (b, 0, i)
    full = lambda b, i: (0, 0)
    out_shape = (
        jax.ShapeDtypeStruct((B, T, 128), BF16), jax.ShapeDtypeStruct((B, T, 128), BF16),
        jax.ShapeDtypeStruct((B, T, 128), BF16), jax.ShapeDtypeStruct((B, T, 128), BF16),
        jax.ShapeDtypeStruct((B, 768, T), BF16), jax.ShapeDtypeStruct((B, 32, T), F32),
        jax.ShapeDtypeStruct((B, T, 128), F32), jax.ShapeDtypeStruct((B, T, 1024), F32),
        jax.ShapeDtypeStruct((B, T, 2048), F32),
    )
    return pl.pallas_call(
        _inproj_kernel,
        grid=(B, nt),
        in_specs=[
            pl.BlockSpec((1, tm, D), row), pl.BlockSpec((1, D), full),
            pl.BlockSpec((D, 512), full), pl.BlockSpec((768, D), full), pl.BlockSpec((32, D), full),
            pl.BlockSpec((D, 128), full), pl.BlockSpec((D, 1024), full), pl.BlockSpec((D, 2048), full),
        ],
        out_specs=(
            pl.BlockSpec((1, tm, 128), row), pl.BlockSpec((1, tm, 128), row),
            pl.BlockSpec((1, tm, 128), row), pl.BlockSpec((1, tm, 128), row),
            pl.BlockSpec((1, 768, tm), col), pl.BlockSpec((1, 32, tm), col),
            pl.BlockSpec((1, tm, 128), row), pl.BlockSpec((1, tm, 1024), row),
            pl.BlockSpec((1, tm, 2048), row),
        ),
        out_shape=out_shape,
        compiler_params=_cparams(("parallel", "parallel")),
        name="inproj",
    )(x, g_mix.reshape(1, D), wkn, wt, wgt, wgn, wxo, wgab)


def _compress_kernel(xk_ref, xv_ref, pek_ref, pev_ref, wklo_ref, wkhi_ref, wk2_ref,
                     wvlo_ref, wvhi_ref, wv2t_ref, kc_ref, vct_ref):
    n_sub = xk_ref.shape[1]

    def hidden(x, pe_ref, wlo_ref, whi_ref):
        wlo = wlo_ref[...]
        whi = whi_ref[...]
        c = _dot(pe_ref[0], wlo) + _dot(pe_ref[1], whi)
        a = _dot(x, wlo)
        b = _dot(x, whi)
        pre = a + pltpu.roll(b, n_sub - 1, 0) + c[0:1, :]
        return jax.nn.gelu(pre).astype(BF16)

    hk = hidden(xk_ref[0], pek_ref, wklo_ref, wkhi_ref)
    kc_ref[0] = _dot(hk, wk2_ref[...]).astype(BF16)
    hv = hidden(xv_ref[0], pev_ref, wvlo_ref, wvhi_ref)
    vct_ref[0] = _dot_nt(wv2t_ref[...], hv).astype(BF16)


def _compress(kcmp, vcmp, pe_k, pe_v, w1k, w2k, w1v, w2v):
    B, T, _ = kcmp.shape
    n_sub = T // CMP_STRIDE
    eye = jnp.eye(G, dtype=F32)

    def big1(w1):
        lo = w1[:CMP_STRIDE * DH].reshape(CMP_STRIDE, DH, PHI)
        hi = w1[CMP_STRIDE * DH:].reshape(CMP_STRIDE, DH, PHI)
        f = lambda a: jnp.einsum('rdp,gh->rgdhp', a, eye).reshape(CMP_STRIDE * G * DH, G * PHI).astype(BF16)
        return f(lo), f(hi)

    def big2(w2):
        return jnp.einsum('pd,gh->gphd', w2, eye).reshape(G * PHI, G * DH)

    def bigpe(pe):
        f = lambda a: jnp.broadcast_to(a[:, None, :], (CMP_STRIDE, G, DH)).reshape(1, CMP_STRIDE * G * DH)
        both = jnp.stack([f(pe[:CMP_STRIDE]), f(pe[CMP_STRIDE:])], axis=0)
        return jnp.broadcast_to(both, (2, SUB, CMP_STRIDE * G * DH)).astype(BF16)

    wklo, wkhi = big1(w1k)
    wvlo, wvhi = big1(w1v)
    wk2 = big2(w2k).astype(BF16)
    wv2t = big2(w2v).T.astype(BF16)
    kw = CMP_STRIDE * G * DH
    xk = kcmp.reshape(B, n_sub, kw)
    xv = vcmp.reshape(B, n_sub, kw)
    c2 = lambda b: (0, 0)
    c3 = lambda b: (0, 0, 0)
    bsel = lambda b: (b, 0, 0)
    return pl.pallas_call(
        _compress_kernel,
        grid=(B,),
        in_specs=[
            pl.BlockSpec((1, n_sub, kw), bsel), pl.BlockSpec((1, n_sub, kw), bsel),
            pl.BlockSpec((2, SUB, kw), c3), pl.BlockSpec((2, SUB, kw), c3),
            pl.BlockSpec((kw, G * PHI), c2), pl.BlockSpec((kw, G * PHI), c2), pl.BlockSpec((G * PHI, G * DH), c2),
            pl.BlockSpec((kw, G * PHI), c2), pl.BlockSpec((kw, G * PHI), c2), pl.BlockSpec((G * DH, G * PHI), c2),
        ],
        out_specs=(pl.BlockSpec((1, n_sub, G * DH), bsel), pl.BlockSpec((1, G * DH, n_sub), bsel)),
        out_shape=(jax.ShapeDtypeStruct((B, n_sub, G * DH), BF16),
                   jax.ShapeDtypeStruct((B, G * DH, n_sub), BF16)),
        compiler_params=_cparams(("parallel",)),
        name="compress",
    )(xk, xv, bigpe(pe_k), bigpe(pe_v), wklo, wkhi, wk2, wvlo, wvhi, wv2t)


def _topk_mask_axis0(score, k):
    n = score.shape[0]
    idx = lax.broadcasted_iota(I32, score.shape, 0)
    work = score
    for _ in range(k):
        m = jnp.max(work, axis=0, keepdims=True)
        first = jnp.min(jnp.where(work == m, idx, n), axis=0, keepdims=True)
        work = jnp.where(idx == first, LOWEST, work)
    return work < 0.5 * LOWEST


def _nsa_kernel(qt_ref, kslc_ref, vslct_ref, kwin_ref, vwint_ref, kc_ref, vct_ref, gt_ref,
                at_ref, e_ref, out_ref, acc_ref, accw_ref, *, tq):
    T = kslc_ref.shape[1]
    n_sub = T // CMP_STRIDE
    nsel = min(NSEL, T // SLC)
    i = pl.program_id(1)
    q0 = i * tq
    t_row = q0 + lax.broadcasted_iota(I32, (1, tq), 1)
    kcv = kc_ref[0]
    cmp_end = lax.broadcasted_iota(I32, (n_sub, 1), 0) * CMP_STRIDE + (2 * CMP_STRIDE - 1)
    bias_c = jnp.where(cmp_end <= t_row, 0.0, NEG)
    col_ok = t_row >= 2 * CMP_STRIDE - 1
    k_loc = lax.broadcasted_iota(I32, (tq, tq), 0)
    q_loc = lax.broadcasted_iota(I32, (tq, tq), 1)
    tri_bias = jnp.where(k_loc <= q_loc, 0.0, NEG)
    prev_bias = jnp.where(k_loc > q_loc + jnp.where(i > 0, 0, tq), 0.0, NEG)
    blk = lax.broadcasted_iota(I32, (NBP, tq), 0)
    cur = t_row // SLC
    at = at_ref[...]
    zeros_half = jnp.zeros((DH, tq), BF16)
    pieces = []
    for g in range(G):
        glo, ghi = g * DH, (g + 1) * DH
        qps = []
        for h in range(HG):
            r0 = (g * HG + h) * DH
            qh = qt_ref[0, r0:r0 + DH, :]
            qps.append(jnp.concatenate([qh, zeros_half] if g == 0 else [zeros_half, qh], axis=0))
        qp_all = jnp.concatenate(qps, axis=1)

        vc_aug = jnp.concatenate([vct_ref[0, glo:ghi, :], jnp.ones((2 * SUB, n_sub), BF16), at], axis=0)
        s_c = _dot(kcv, qp_all)
        imp = jnp.zeros((NBP, tq), F32)
        o_cmp = []
        for h in range(HG):
            s_h = s_c[:, h * tq:(h + 1) * tq] + bias_c
            p = jnp.exp2(s_h - jnp.max(s_h, axis=0, keepdims=True)).astype(BF16)
            r = _dot(vc_aug, p)
            inv = jnp.where(col_ok, 1.0 / r[DH:DH + 1], 0.0)
            o_cmp.append(r[0:DH] * inv)
            imp = imp + r[DH + 2 * SUB:] * inv
        forced = jnp.where(blk == 0, 0.0, jnp.where(blk == cur, 0.0, jnp.where(blk == cur - 1, 0.0, NEG)))
        free = jnp.where(blk <= cur, jnp.where(forced < 0.0, imp, NEG), NEG)
        picked = _topk_mask_axis0(free, nsel - 3)
        selbias = jnp.where(picked, 0.0, forced).astype(BF16)
        q_aug = jnp.concatenate([qp_all, jnp.concatenate([selbias] * HG, axis=1)], axis=0)
        def chunk(k_ref, vt_ref, acc, rhs, c, ms, bias, glo=glo, ghi=ghi, kc=tq):
            k0 = pl.multiple_of(c * kc, kc)
            lhs = k_ref[0, pl.ds(k0, kc), :]
            if rhs.shape[0] == 2 * LANE:
                lhs = jnp.concatenate([lhs, e_ref[pl.ds(k0, kc), :]], axis=1)
            v_aug = jnp.concatenate([vt_ref[0, glo:ghi, pl.ds(k0, kc)], jnp.ones((2 * SUB, kc), BF16)], axis=0)
            s = _dot(lhs, rhs)
            new_ms = []
            for h in range(HG):
                s_h = s[:, h * tq:(h + 1) * tq]
                if bias is not None:
                    s_h = s_h + bias
                m_new = jnp.maximum(ms[h], jnp.max(s_h, axis=0, keepdims=True))
                alpha = jnp.exp2(ms[h] - m_new)
                p = jnp.exp2(s_h - m_new).astype(BF16)
                acc[h] = alpha * acc[h] + _dot(v_aug, p)
                new_ms.append(m_new)
            return tuple(new_ms)

        ms0 = tuple(jnp.full((1, tq), NEG, F32) for _ in range(HG))
        acc_ref[...] = jnp.zeros(acc_ref.shape, F32)
        ms = lax.fori_loop(0, i // 2, lambda c, ms, q_aug=q_aug: chunk(kslc_ref, vslct_ref, acc_ref, q_aug, c, ms,
                                                                       None, kc=2 * tq), ms0)
        ms = lax.fori_loop(2 * (i // 2), i,
                           lambda c, ms, q_aug=q_aug: chunk(kslc_ref, vslct_ref, acc_ref, q_aug, c, ms, None), ms)
        chunk(kslc_ref, vslct_ref, acc_ref, q_aug, i, ms, tri_bias)
        accw_ref[...] = jnp.zeros(accw_ref.shape, F32)
        ms = chunk(kwin_ref, vwint_ref, accw_ref, qp_all, i, ms0, tri_bias)
        chunk(kwin_ref, vwint_ref, accw_ref, qp_all, jnp.maximum(i - 1, 0), ms, prev_bias)
        for h in range(HG):
            accv = acc_ref[h]
            o_slc = accv[0:DH] * (1.0 / accv[DH:DH + 1])
            accv = accw_ref[h]
            o_win = accv[0:DH] * (1.0 / accv[DH:DH + 1])
            gi = (g * HG + h) * 3
            sg = jax.nn.sigmoid(gt_ref[0, gi:gi + 3, :])
            pieces.append(sg[0:1] * o_cmp[h] + sg[1:2] * o_slc + sg[2:3] * o_win)
    yt = jnp.concatenate(pieces, axis=0)
    out_ref[0] = yt.T.astype(BF16)


def _nsa(tt, kslc, kwin, kc, vct, gt, tq):
    B, T, _ = kslc.shape
    n_sub = T // CMP_STRIDE
    nb = T // SLC
    assert nb <= NBP and T % tq == 0
    assert tq == WIN
    cidx = jnp.arange(n_sub)[None, :]
    bidx = jnp.arange(NBP)[:, None]
    off = cidx - (SLC // CMP_STRIDE) * bidx
    at = (jnp.where((off >= -1) & (off <= 3), 1.0, 0.0)
          + jnp.where((off >= 0) & (off <= 2), 1.0, 0.0))
    at = jnp.where((cidx < n_sub - 1) & (bidx < nb), at, 0.0).astype(BF16)
    e = (jnp.arange(T)[:, None] // SLC == jnp.arange(NBP)[None, :]).astype(BF16)
    nt = T // tq
    kern = functools.partial(_nsa_kernel, tq=tq)
    bfull = lambda b, i: (b, 0, 0)
    return pl.pallas_call(
        kern,
        grid=(B, nt),
        in_specs=[
            pl.BlockSpec((1, 512, tq), lambda b, i: (b, 0, i)),
            pl.BlockSpec((1, T, 128), bfull),
            pl.BlockSpec((1, 128, T), lambda b, i: (b, 4, 0)),
            pl.BlockSpec((1, T, 128), bfull),
            pl.BlockSpec((1, 128, T), lambda b, i: (b, 5, 0)),
            pl.BlockSpec((1, n_sub, 128), bfull),
            pl.BlockSpec((1, 128, n_sub), bfull),
            pl.BlockSpec((1, 32, tq), lambda b, i: (b, 0, i)),
            pl.BlockSpec((NBP, n_sub), lambda b, i: (0, 0)),
            pl.BlockSpec((T, NBP), lambda b, i: (0, 0)),
        ],
        out_specs=pl.BlockSpec((1, tq, 512), lambda b, i: (b, i, 0)),
        out_shape=jax.ShapeDtypeStruct((B, T, 512), BF16),
        scratch_shapes=[pltpu.VMEM((HG, DH + 2 * SUB, tq), F32), pltpu.VMEM((HG, DH + 2 * SUB, tq), F32)],
        compiler_params=_cparams(("parallel", "parallel")),
        name="nsa",
    )(tt, kslc, tt, kwin, tt, kc, vct, gt, at, e)


def _log_sigmoid(x):
    return jnp.minimum(x, 0.0) - jnp.log(1.0 + jnp.exp(-jnp.abs(x)))


def _mlstm_kernel(xo_ref, gn_ref, gt_ref, bn_ref, bt_ref, wconv_ref, bconv_ref, wq_ref, wk_ref, wv_ref,
                  gain_ref, skip_ref, tri_ref, trit_ref, out_ref, xbuf, caug, mst, *, L):
    c = pl.program_id(1)

    @pl.when(c == 0)
    def _():
        xbuf[...] = jnp.zeros(xbuf.shape, F32)
        caug[...] = jnp.zeros(caug.shape, F32)
        mst[...] = jnp.zeros(mst.shape, F32)

    xbuf[0:SUB, :] = xbuf[L:L + SUB, :]
    xbuf[SUB:SUB + L, :] = xo_ref[0, :, 0:MI]
    conv = bconv_ref[...] + wconv_ref[3:4, :] * xbuf[SUB:SUB + L, :]
    for r in range(3):
        s = 3 - r
        conv = conv + wconv_ref[r:r + 1, :] * xbuf[SUB - s:SUB - s + L, :]
    xc = conv * jax.nn.sigmoid(conv)

    gn = gn_ref[0] + bn_ref[...]
    gt = gt_ref[0] + bt_ref[...]
    lf_hi, lf_lo = _split(_log_sigmoid(gn))
    tri = tri_ref[...]
    b_col_all = _dot(tri, lf_hi) + _dot(tri, lf_lo)
    lft_hi, lft_lo = _split(_log_sigmoid(gt))
    trit = trit_ref[...]
    b_row_all = _dot(lft_hi, trit) + _dot(lft_lo, trit)
    ti = lax.broadcasted_iota(I32, (L, L), 0)
    si = lax.broadcasted_iota(I32, (L, L), 1)
    causal = si <= ti
    one_col = jnp.where(lax.broadcasted_iota(I32, (L, DHB), 1) == 0, 1.0, 0.0).astype(BF16)

    outs = []
    for h in range(HB):
        lo, hi = h * DHB, (h + 1) * DHB
        xch = xc[:, lo:hi]
        xcb = xch.astype(BF16)
        q = _dot(xcb, wq_ref[h]).astype(BF16)
        k = _dot(xcb, wk_ref[h]) * (DHB ** -0.5)
        v = _dot(xo_ref[0, :, lo:hi].astype(BF16), wv_ref[h]).astype(BF16)
        v_aug = jnp.concatenate([v, one_col], axis=1)
        bc = b_col_all[:, 28 + h:29 + h]
        lic = gn[:, 24 + h:25 + h]
        br = b_row_all[28 + h:29 + h, :]
        lir = gt[24 + h:25 + h, :]
        m_prev = mst[h, 0:1, 0:1]
        log_d = jnp.where(causal, bc - br + lir, NEG)
        m_inter = bc + m_prev
        m_t = jnp.maximum(jnp.max(log_d, axis=1, keepdims=True), m_inter)
        dmat = jnp.exp(log_d - m_t)
        s = (_dot_nt(q, k.astype(BF16)) * dmat).astype(BF16)
        inter = jnp.exp(m_inter - m_t)
        c_old = caug[h]
        r = _dot(s, v_aug) + inter * _dot(q, c_old.astype(BF16))
        num = r[:, 0:DHB]
        den = r[:, DHB:DHB + 1]
        hval = num / jnp.maximum(jnp.abs(den), jnp.exp(-m_t))
        b_last = bc[L - 1:L, :]
        log_w = b_last - bc + lic
        m_new = jnp.maximum(b_last + m_prev, jnp.max(log_w, axis=0, keepdims=True))
        w = jnp.exp(log_w - m_new)
        decay = jnp.exp(b_last + m_prev - m_new)
        kwt = (k * w).T.astype(BF16)
        caug[h] = decay * c_old + _dot(kwt, v_aug)
        mst[h] = jnp.broadcast_to(m_new, (SUB, LANE))
        o = jax.nn.sigmoid(xo_ref[0, :, MI + lo:MI + hi]) * hval
        mu = jnp.mean(o, axis=-1, keepdims=True)
        var = jnp.mean((o - mu) ** 2, axis=-1, keepdims=True)
        hn = (o - mu) * lax.rsqrt(var + EPS) * gain_ref[:, lo:hi]
        outs.append(hn + skip_ref[:, lo:hi] * xch)
    out_ref[0] = jnp.concatenate(outs, axis=1).astype(BF16)


def _mlstm(xo, gn, gt, w_conv, b_conv, w_q, w_k, w_v, b_i, b_f, mh_gain, skip, L):
    B, T, _ = xo.shape
    bias = jnp.concatenate([jnp.zeros((24,), F32), b_i, b_f])
    bn = jnp.pad(bias, (0, LANE - 32)).reshape(1, LANE)
    bt = bias.reshape(32, 1)
    tri = (jnp.arange(L)[None, :] <= jnp.arange(L)[:, None]).astype(BF16)
    c2 = lambda b, c: (0, 0)
    c3 = lambda b, c: (0, 0, 0)
    return pl.pallas_call(
        functools.partial(_mlstm_kernel, L=L),
        grid=(B, T // L),
        in_specs=[
            pl.BlockSpec((1, L, 2 * MI), lambda b, c: (b, c, 0)),
            pl.BlockSpec((1, L, LANE), lambda b, c: (b, c, 0)),
            pl.BlockSpec((1, 32, L), lambda b, c: (b, 0, c)),
            pl.BlockSpec((1, LANE), c2), pl.BlockSpec((32, 1), c2),
            pl.BlockSpec((4, MI), c2), pl.BlockSpec((1, MI), c2),
            pl.BlockSpec((HB, DHB, DHB), c3), pl.BlockSpec((HB, DHB, DHB), c3), pl.BlockSpec((HB, DHB, DHB), c3),
            pl.BlockSpec((1, MI), c2), pl.BlockSpec((1, MI), c2),
            pl.BlockSpec((L, L), c2), pl.BlockSpec((L, L), c2),
        ],
        out_specs=pl.BlockSpec((1, L, MI), lambda b, c: (b, c, 0)),
        out_shape=jax.ShapeDtypeStruct((B, T, MI), BF16),
        scratch_shapes=[pltpu.VMEM((L + 2 * SUB, MI), F32), pltpu.VMEM((HB, DHB, 2 * DHB), F32),
                        pltpu.VMEM((HB, SUB, LANE), F32)],
        compiler_params=_cparams(("parallel", "arbitrary")),
        name="mlstm",
    )(xo, gn, gt, bn, bt, w_conv, b_conv.reshape(1, MI), w_q.astype(BF16), w_k.astype(BF16),
      w_v.astype(BF16), mh_gain.reshape(1, MI), skip.reshape(1, MI), tri, tri.T)


def _memkv_kernel(m_ref, g_ref, wk_ref, wv_ref, k_ref, v_ref):
    hb = _rms(m_ref[0], g_ref[...]).astype(BF16)
    k_ref[0] = _dot(hb, wk_ref[...]).astype(BF16)
    v_ref[0] = _dot(hb, wv_ref[...]).astype(BF16)


def _memkv(mem, g_mem, w_k, w_v):
    B, M, _ = mem.shape
    c2 = lambda b: (0, 0)
    bs = lambda b: (b, 0, 0)
    return pl.pallas_call(
        _memkv_kernel,
        grid=(B,),
        in_specs=[pl.BlockSpec((1, M, D), bs), pl.BlockSpec((1, D), c2),
                  pl.BlockSpec((D, D), c2), pl.BlockSpec((D, D), c2)],
        out_specs=(pl.BlockSpec((1, M, D), bs), pl.BlockSpec((1, M, D), bs)),
        out_shape=(jax.ShapeDtypeStruct((B, M, D), BF16), jax.ShapeDtypeStruct((B, M, D), BF16)),
        compiler_params=_cparams(("parallel",)),
        name="memkv",
    )(mem, g_mem.reshape(1, D), w_k.astype(BF16), w_v.astype(BF16))


def _mid_kernel(x_ref, ya_ref, yb_ref, gab_ref, wua_ref, wub_ref, wo_ref, gx_ref, wq_ref, kx_ref, vx_ref,
                wox_ref, gf_ref, wrh_ref, wrl_ref, br_ref,
                x2_ref, h3_ref, ti_ref, tw_ref, cnt_ref, *, tm):
    a = _dot(ya_ref[0], wua_ref[...])
    b = _dot(yb_ref[0], wub_ref[...])
    merged = jax.nn.sigmoid(gab_ref[0, :, 0:D]) * a + jax.nn.sigmoid(gab_ref[0, :, D:2 * D]) * b
    x1 = x_ref[0] + _dot(merged.astype(BF16), wo_ref[...])
    q = (_dot(_rms(x1, gx_ref[...]).astype(BF16), wq_ref[...]) * (DHX ** -0.5)).astype(BF16)
    kx = kx_ref[0]
    vx = vx_ref[0]
    heads = []
    for h in range(HX):
        lo, hi = h * DHX, (h + 1) * DHX
        s = _dot_nt(q[:, lo:hi], kx[:, lo:hi])
        m = jnp.max(s, axis=-1, keepdims=True)
        e = jnp.exp(s - m)
        p = e * (1.0 / jnp.sum(e, axis=-1, keepdims=True))
        heads.append(_dot(p.astype(BF16), vx[:, lo:hi]).astype(BF16))
    x2 = x1 + _dot(jnp.concatenate(heads, axis=1), wox_ref[...])
    x2_ref[0] = x2
    h3 = _rms(x2, gf_ref[...])
    h3_ref[0] = h3
    h_hi, h_lo = _split(h3)
    logits = (_dot(h_hi, wrh_ref[...]) + _dot(h_lo, wrh_ref[...]) + _dot(h_hi, wrl_ref[...])) + br_ref[...]
    lane = lax.broadcasted_iota(I32, (tm, LANE), 1)
    work = logits
    vals, idxs = [], []
    multi = jnp.zeros((tm, LANE), F32)
    for _ in range(TOPK):
        m = jnp.max(work, axis=-1, keepdims=True)
        first = jnp.min(jnp.where(work == m, lane, LANE), axis=-1, keepdims=True)
        pick = lane == first
        vals.append(m)
        idxs.append(first)
        multi = jnp.where(pick, 1.0, multi)
        work = jnp.where(pick, LOWEST, work)
    es = [jnp.exp(v - vals[0]) for v in vals]
    inv = 1.0 / (es[0] + es[1] + es[2] + es[3])
    tw = jnp.zeros((tm, LANE), F32)
    for k in range(TOPK):
        tw = jnp.where(lane == k, es[k] * inv, tw)
    tw_ref[0] = tw
    ti = jnp.zeros((tm, LANE), I32)
    for k in range(TOPK):
        ti = jnp.where(lane == k, idxs[k], ti)
    ti_ref[0] = ti
    cnt_ref[0] = jnp.broadcast_to(jnp.sum(multi, axis=0, keepdims=True), (SUB, LANE))


def _mid(x, ya, yb, gab, w_up_a, w_up_b, w_o_mix, g_x, w_q_x, kx, vx, w_o_x, g_ffn, w_router, b_router, tm):
    B, T, _ = x.shape
    M = kx.shape[1]
    wr = jnp.pad(w_router, ((0, 0), (0, LANE - NE)))
    wrh, wrl = _split(wr)
    br = jnp.concatenate([b_router, jnp.full((LANE - NE,), NEG, F32)]).reshape(1, LANE)
    c2 = lambda b, i: (0, 0)
    row = lambda b, i: (b, i, 0)
    bs = lambda b, i: (b, 0, 0)
    nt = T // tm
    return pl.pallas_call(
        functools.partial(_mid_kernel, tm=tm),
        grid=(B, nt),
        in_specs=[
            pl.BlockSpec((1, tm, D), row), pl.BlockSpec((1, tm, 512), row), pl.BlockSpec((1, tm, MI), row),
            pl.BlockSpec((1, tm, 2 * D), row),
            pl.BlockSpec((512, D), c2), pl.BlockSpec((MI, D), c2), pl.BlockSpec((D, D), c2),
            pl.BlockSpec((1, D), c2), pl.BlockSpec((D, D), c2),
            pl.BlockSpec((1, M, D), bs), pl.BlockSpec((1, M, D), bs),
            pl.BlockSpec((D, D), c2), pl.BlockSpec((1, D), c2),
            pl.BlockSpec((D, LANE), c2), pl.BlockSpec((D, LANE), c2), pl.BlockSpec((1, LANE), c2),
        ],
        out_specs=(
            pl.BlockSpec((1, tm, D), row), pl.BlockSpec((1, tm, D), row),
            pl.BlockSpec((1, tm, LANE), row), pl.BlockSpec((1, tm, LANE), row),
            pl.BlockSpec((1, SUB, LANE), lambda b, i: (b * nt + i, 0, 0)),
        ),
        out_shape=(
            jax.ShapeDtypeStruct((B, T, D), F32), jax.ShapeDtypeStruct((B, T, D), F32),
            jax.ShapeDtypeStruct((B, T, LANE), I32), jax.ShapeDtypeStruct((B, T, LANE), F32),
            jax.ShapeDtypeStruct((B * nt, SUB, LANE), F32),
        ),
        compiler_params=_cparams(("parallel", "parallel")),
        name="mid",
    )(x, ya, yb, gab, w_up_a.astype(BF16), w_up_b.astype(BF16), w_o_mix.astype(BF16), g_x.reshape(1, D),
      w_q_x.astype(BF16), kx, vx, w_o_x.astype(BF16), g_ffn.reshape(1, D), wrh, wrl, br)


def _local_capacity(tm):
    return TOPK * tm + NE * SUB


def _local_rows(ti, ltri, utri):
    tm = ti.shape[0]
    lane = lax.broadcasted_iota(I32, (tm, LANE), 1)
    hots = [lane == ti[:, k:k + 1] for k in range(TOPK)]
    multi = jnp.zeros((tm, LANE), F32)
    for hot in hots:
        multi = jnp.where(hot, 1.0, multi)
    prior = _dot(ltri, multi.astype(BF16))
    cnt = jnp.broadcast_to(jnp.sum(multi, axis=0, keepdims=True), (SUB, LANE))
    cnt = jnp.ceil(cnt * (1.0 / SUB)) * SUB
    loc = _dot(cnt.astype(BF16), utri)[0:1]
    base = prior + loc
    return [jnp.sum(jnp.where(hot, base, 0.0), axis=-1, keepdims=True) for hot in hots]


def _run_copies(cnt_ref, gs_ref, tile, make_copy):
    def body(e, local):
        n = pl.multiple_of(cnt_ref[tile * NE + e], SUB)

        @pl.when(n > 0)
        def _():
            make_copy(pl.multiple_of(local, SUB), pl.multiple_of(gs_ref[tile * NE + e], SUB), n).start()

        return local + n

    lax.fori_loop(0, NE, body, 0)


def _dispatch_kernel(cnt_ref, gs_ref, rows_ref, ps_ref, pn_ref, na_ref, h3_ref, ti_ref, ltri_ref, utri_ref,
                     xbuf_hbm, xl, zbuf, sem, zsem, *, tm, nt, n_blk):
    j = pl.program_id(0)
    slot = j % 2
    rows = xl.shape[1]

    def wait_tile(s, tile):
        n = pl.multiple_of(rows_ref[tile], SUB)
        pltpu.make_async_copy(xl.at[s, pl.ds(0, n), :], xbuf_hbm.at[pl.ds(0, n), :], sem.at[s]).wait()

    def pad_copy(e):
        n = pl.multiple_of(pn_ref[e], SUB)
        return pltpu.make_async_copy(zbuf.at[pl.ds(0, n), :],
                                     xbuf_hbm.at[pl.ds(pl.multiple_of(ps_ref[e], SUB), n), :], zsem)

    def tail_copy(b):
        return pltpu.make_async_copy(zbuf, xbuf_hbm.at[pl.ds(pl.multiple_of(b * MBLK, MBLK), MBLK), :], zsem)

    def for_tail_blocks(fn):
        def body(b, carry):
            fn(b)
            return carry

        lax.fori_loop(na_ref[0], n_blk, body, 0)

    @pl.when(j >= 2)
    def _():
        wait_tile(slot, j - 2)

    @pl.when(j == 0)
    def _():
        zbuf[...] = jnp.zeros(zbuf.shape, F32)
        for e in range(NE):
            @pl.when(pn_ref[e] > 0)
            def _(e=e):
                pad_copy(e).start()
        for_tail_blocks(lambda b: tail_copy(b).start())

    lrs = _local_rows(ti_ref[...], ltri_ref[...], utri_ref[...])
    lane = lax.broadcasted_iota(I32, (tm, LANE), 1)
    lmat = jnp.zeros((tm, LANE), F32)
    for k in range(TOPK):
        lmat = jnp.where(lane == k, lrs[k], lmat)
    lt = lmat.T
    rio = lax.broadcasted_iota(I32, (rows, 1), 0).astype(F32)
    st = jnp.zeros((rows, tm), F32)
    for k in range(TOPK):
        st = jnp.where(rio == lt[k:k + 1, :], 1.0, st)
    xl[slot] = _dot(st.astype(BF16), h3_ref[...].astype(BF16))
    _run_copies(cnt_ref, gs_ref, j,
                lambda l, g, n: pltpu.make_async_copy(xl.at[slot, pl.ds(l, n), :], xbuf_hbm.at[pl.ds(g, n), :],
                                                      sem.at[slot]))

    @pl.when(j == nt - 1)
    def _():
        wait_tile(slot, j)
        if nt >= 2:
            wait_tile(1 - slot, j - 1)
        for e in range(NE):
            @pl.when(pn_ref[e] > 0)
            def _(e=e):
                pad_copy(e).wait()
        for_tail_blocks(lambda b: tail_copy(b).wait())


def _dispatch(h3, ti, plan, n_slots, tm):
    n_tok = h3.shape[0]
    nt = n_tok // tm
    ltri = (jnp.arange(tm)[None, :] < jnp.arange(tm)[:, None]).astype(BF16)
    utri = (jnp.arange(LANE)[:, None] < jnp.arange(LANE)[None, :]).astype(BF16)
    c2 = lambda j, *_: (0, 0)
    grid_spec = pltpu.PrefetchScalarGridSpec(
        num_scalar_prefetch=6,
        grid=(nt,),
        in_specs=[
            pl.BlockSpec((tm, D), lambda j, *_: (j, 0)), pl.BlockSpec((tm, LANE), lambda j, *_: (j, 0)),
            pl.BlockSpec((tm, tm), c2), pl.BlockSpec((LANE, LANE), c2),
        ],
        out_specs=pl.BlockSpec(memory_space=pl.ANY),
        scratch_shapes=[pltpu.VMEM((2, _local_capacity(tm), D), F32), pltpu.VMEM((MBLK, D), F32),
                        pltpu.SemaphoreType.DMA((2,)), pltpu.SemaphoreType.DMA(())],
    )
    return pl.pallas_call(
        functools.partial(_dispatch_kernel, tm=tm, nt=nt, n_blk=n_slots // MBLK),
        grid_spec=grid_spec,
        out_shape=jax.ShapeDtypeStruct((n_slots, D), F32),
        compiler_params=_cparams(("arbitrary",)),
        name="dispatch",
    )(plan["cnt"], plan["gstart"], plan["rows"], plan["pad_start"], plan["pad_len"], plan["nact"],
      h3, ti, ltri, utri)


def _expert_kernel(be_ref, na_ref, x_ref, wgu_ref, bgu_ref, wdn_ref, bdn_ref, out_ref, wgu_bf, wdn_bf):
    i = pl.program_id(0)
    nact = na_ref[0]

    @pl.when(i < nact)
    def _():
        @pl.when(jnp.logical_or(i == 0, be_ref[i] != be_ref[jnp.maximum(i - 1, 0)]))
        def _():
            wgu_bf[...] = wgu_ref[0].astype(BF16)
            wdn_bf[...] = wdn_ref[0].astype(BF16)

        gu = _dot(x_ref[...].astype(BF16), wgu_bf[...]) + bgu_ref[0]
        g = jnp.minimum(gu[:, 0:DFF], LIMIT)
        u = jnp.clip(gu[:, DFF:2 * DFF], -LIMIT, LIMIT)
        act = (u + 1.0) * (g * jax.nn.sigmoid(ALPHA * g))
        out_ref[...] = _dot(act.astype(BF16), wdn_bf[...]) + bdn_ref[0]

    @pl.when(i >= nact)
    def _():
        out_ref[...] = jnp.zeros(out_ref.shape, F32)


def _experts(xbuf, plan, w_gu, b_gu, w_down, b_down):
    n_blk = plan["blk_exp"].shape[0]
    wsel = lambda i, be, na: (be[i], 0, 0)
    grid_spec = pltpu.PrefetchScalarGridSpec(
        num_scalar_prefetch=2,
        grid=(n_blk,),
        in_specs=[
            pl.BlockSpec((MBLK, D), lambda i, be, na: (jnp.minimum(i, na[0] - 1), 0)),
            pl.BlockSpec((1, D, 2 * DFF), wsel), pl.BlockSpec((1, 1, 2 * DFF), wsel),
            pl.BlockSpec((1, DFF, D), wsel), pl.BlockSpec((1, 1, D), wsel),
        ],
        out_specs=pl.BlockSpec((MBLK, D), lambda i, be, na: (i, 0)),
        scratch_shapes=[pltpu.VMEM((D, 2 * DFF), BF16), pltpu.VMEM((DFF, D), BF16)],
    )
    return pl.pallas_call(
        _expert_kernel,
        grid_spec=grid_spec,
        out_shape=jax.ShapeDtypeStruct((n_blk * MBLK, D), F32),
        compiler_params=_cparams(("arbitrary",)),
        name="experts",
    )(plan["blk_exp"], plan["nact"], xbuf, w_gu, b_gu.reshape(NE, 1, 2 * DFF), w_down, b_down.reshape(NE, 1, D))


def _combine_kernel(cnt_ref, gs_ref, rows_ref, x2_ref, ti_ref, tw_ref, gfin_ref, ltri_ref, utri_ref, y_hbm,
                    out_ref, yl, sem, *, tm, nt, final_norm):
    j = pl.program_id(0)
    slot = j % 2
    rows = yl.shape[1]

    def issue(tile, s):
        _run_copies(cnt_ref, gs_ref, tile,
                    lambda l, g, n: pltpu.make_async_copy(y_hbm.at[pl.ds(g, n), :], yl.at[s, pl.ds(l, n), :],
                                                          sem.at[s]))

    @pl.when(j == 0)
    def _():
        yl[...] = jnp.zeros(yl.shape, F32)
        issue(0, 0)

    n_rows = pl.multiple_of(rows_ref[j], SUB)
    pltpu.make_async_copy(y_hbm.at[pl.ds(0, n_rows), :], yl.at[slot, pl.ds(0, n_rows), :], sem.at[slot]).wait()

    @pl.when(j + 1 < nt)
    def _():
        issue(j + 1, 1 - slot)

    lrs = _local_rows(ti_ref[...], ltri_ref[...], utri_ref[...])
    rio = lax.broadcasted_iota(I32, (1, rows), 1).astype(F32)
    sw = jnp.zeros((tm, rows), F32)
    for k in range(TOPK):
        sw = jnp.where(rio == lrs[k], tw_ref[:, k:k + 1], sw)
    yv = yl[slot].astype(BF16)
    s_hi, s_lo = _split(sw)
    acc = x2_ref[...] + (_dot(s_hi, yv) + _dot(s_lo, yv))
    if final_norm:
        acc = _rms(acc, gfin_ref[...])
    out_ref[...] = acc


def _combine(plan, x2, ti, tw, g_final, ybuf, tm, final_norm):
    n_tok = x2.shape[0]
    nt = n_tok // tm
    ltri = (jnp.arange(tm)[None, :] < jnp.arange(tm)[:, None]).astype(BF16)
    utri = (jnp.arange(LANE)[:, None] < jnp.arange(LANE)[None, :]).astype(BF16)
    c2 = lambda j, *_: (0, 0)
    tile = lambda j, *_: (j, 0)
    grid_spec = pltpu.PrefetchScalarGridSpec(
        num_scalar_prefetch=3,
        grid=(nt,),
        in_specs=[
            pl.BlockSpec((tm, D), tile), pl.BlockSpec((tm, LANE), tile), pl.BlockSpec((tm, LANE), tile),
            pl.BlockSpec((1, D), c2), pl.BlockSpec((tm, tm), c2), pl.BlockSpec((LANE, LANE), c2),
            pl.BlockSpec(memory_space=pl.ANY),
        ],
        out_specs=pl.BlockSpec((tm, D), tile),
        scratch_shapes=[pltpu.VMEM((2, _local_capacity(tm), D), F32), pltpu.SemaphoreType.DMA((2,))],
    )
    return pl.pallas_call(
        functools.partial(_combine_kernel, tm=tm, nt=nt, final_norm=final_norm),
        grid_spec=grid_spec,
        out_shape=jax.ShapeDtypeStruct((n_tok, D), F32),
        compiler_params=_cparams(("arbitrary",)),
        name="combine",
    )(plan["cnt"], plan["gstart"], plan["rows"], x2, ti, tw, g_final.reshape(1, D), ltri, utri, ybuf)


def _moe_plan(tile_cnt, n_tok):
    n_tiles = tile_cnt.shape[0]
    cnt = (tile_cnt[:, 0, 0:NE].astype(I32) + SUB - 1) // SUB * SUB
    before = jnp.cumsum(cnt, axis=0) - cnt
    total = jnp.sum(cnt, axis=0)
    padded = (total + MBLK - 1) // MBLK * MBLK
    p_end = jnp.cumsum(padded)
    p_start = p_end - padded
    n_blk = (n_tok * TOPK + n_tiles * NE * (SUB - 1) + MBLK - 1) // MBLK + NE
    blk_start = jnp.arange(n_blk, dtype=I32) * MBLK
    return {
        "cnt": cnt.reshape(-1),
        "rows": jnp.sum(cnt, axis=1).astype(I32),
        "gstart": (p_start[None, :] + before).reshape(-1).astype(I32),
        "pad_start": (p_start + total).astype(I32),
        "pad_len": (padded - total).astype(I32),
        "blk_exp": jnp.minimum(jnp.sum(p_end[None, :] <= blk_start[:, None], axis=1), NE - 1).astype(I32),
        "nact": (p_end[-1] // MBLK).astype(I32).reshape(1),
        "n_slots": n_blk * MBLK,
    }


def _layer(x, mem, g_mix, w_in, pe_cmp_k, pe_cmp_v, w_phi1_k, w_phi2_k, w_phi1_v, w_phi2_v,
           w_conv, b_conv, w_q_m, w_k_m, w_v_m, b_i, b_f, mh_gain, skip_m,
           w_up_a, w_up_b, w_o_mix, g_x, g_mem, w_q_x, w_k_x, w_v_x, w_o_x,
           g_ffn, w_router, b_router, w_gu, b_gu, w_down, b_down, g_final, final_norm):
    B, T, _ = x.shape
    kcmp, vcmp, kslc, kwin, tt, gt, gn, xo, gab = _inproj(x, g_mix, w_in, tm=256)
    kc, vct = _compress(kcmp, vcmp, pe_cmp_k, pe_cmp_v, w_phi1_k, w_phi2_k, w_phi1_v, w_phi2_v)
    ya = _nsa(tt, kslc, kwin, kc, vct, gt, tq=WIN)
    yb = _mlstm(xo, gn, gt, w_conv, b_conv, w_q_m, w_k_m, w_v_m, b_i, b_f, mh_gain, skip_m, L=256)
    kx, vx = _memkv(mem, g_mem, w_k_x, w_v_x)
    x2, h3, ti, tw, cnt = _mid(x, ya, yb, gab, w_up_a, w_up_b, w_o_mix, g_x, w_q_x, kx, vx, w_o_x,
                               g_ffn, w_router, b_router, tm=256)
    n_tok = B * T
    plan = _moe_plan(cnt, n_tok)
    ti = ti.reshape(n_tok, LANE)
    xbuf = _dispatch(h3.reshape(n_tok, D), ti, plan, plan["n_slots"], tm=256)
    ybuf = _experts(xbuf, plan, w_gu, b_gu, w_down, b_down)
    out = _combine(plan, x2.reshape(n_tok, D), ti, tw.reshape(n_tok, LANE), g_final, ybuf, tm=256,
                   final_norm=final_norm)
    return out.reshape(B, T, D)


def kernel(x, mem, g_mix, w_in, pe_cmp_k, pe_cmp_v, w_phi1_k, w_phi2_k, w_phi1_v, w_phi2_v, w_conv, b_conv, w_q_m, w_k_m, w_v_m, b_i, b_f, mh_gain, skip_m, w_up_a, w_up_b, w_o_mix, g_x, g_mem, w_q_x, w_k_x, w_v_x, w_o_x, g_ffn, w_router, b_router, w_gu, b_gu, w_down, b_down, g_final):
    layers = (g_mix, w_in, pe_cmp_k, pe_cmp_v, w_phi1_k, w_phi2_k, w_phi1_v, w_phi2_v, w_conv, b_conv,
              w_q_m, w_k_m, w_v_m, b_i, b_f, mh_gain, skip_m, w_up_a, w_up_b, w_o_mix, g_x, g_mem,
              w_q_x, w_k_x, w_v_x, w_o_x, g_ffn, w_router, b_router, w_gu, b_gu, w_down, b_down)
    depth = g_mix.shape[0]
    for l in range(depth):
        x = _layer(x, mem, *(w[l] for w in layers), g_final, final_norm=(l == depth - 1))
    return x
```

```python
import functools

import jax
import jax.numpy as jnp
from jax import lax
from jax.experimental import pallas as pl
from jax.experimental.pallas import tpu as pltpu

F32 = jnp.float32
BF16 = jnp.bfloat16
I32 = jnp.int32

D = 1024
EPS = 1e-5
NEG = -1e30
BIG = 1e30
LOWEST = -3e38

G = 2
HG = 4
DH = 64
CMP_STRIDE = 16
SLC = 64
NSEL = 16
WIN = 512
PHI = 128
NBP = 128
LOG2E = 1.4426950408889634
HB = 4
DHB = 128
MI = HB * DHB
HX = 4
DHX = D // HX
NE = 32
TOPK = 4
DFF = D
LIMIT = 7.0
ALPHA = 1.702
MBLK = 256

LANE = 128
SUB = 8
VMEM_LIMIT = 56 * 1024 * 1024

_O_Q = 0
_O_KCMP = 512
_O_VCMP = 640
_O_KSLC = 768
_O_VSLC = 896
_O_KWIN = 1024
_O_VWIN = 1152
_O_GNSA = 1280
_O_XM = 1304
_O_OPRE = 1816
_O_IPRE = 2328
_O_FPRE = 2332
_O_GA = 2336
_O_GB = 3360


def _cparams(sem):
    return pltpu.CompilerParams(dimension_semantics=sem, vmem_limit_bytes=VMEM_LIMIT)


def _rms(x, g):
    ms = jnp.mean(x * x, axis=-1, keepdims=True)
    return x * lax.rsqrt(ms + EPS) * g


def _dot(a, b):
    return jnp.dot(a, b, preferred_element_type=F32)


def _dot_nt(a, b):
    return lax.dot_general(a, b, (((1,), (1,)), ((), ())), preferred_element_type=F32)


def _split(x):
    hi = x.astype(BF16)
    lo = (x - hi.astype(F32)).astype(BF16)
    return hi, lo


def _inproj_kernel(x_ref, g_ref, wkn_ref, wt_ref, wgt_ref, wgn_ref, wxo_ref, wgab_ref,
                   kcmp_ref, vcmp_ref, kslc_ref, kwin_ref, tt_ref, gt_ref, gn_ref, xo_ref, gab_ref):
    hb = _rms(x_ref[0], g_ref[...]).astype(BF16)
    kn = _dot(hb, wkn_ref[...]).astype(BF16)
    kcmp_ref[0] = kn[:, 0:128]
    vcmp_ref[0] = kn[:, 128:256]
    kslc_ref[0] = kn[:, 256:384]
    kwin_ref[0] = kn[:, 384:512]
    tt_ref[0] = _dot_nt(wt_ref[...], hb).astype(BF16)
    gt_ref[0] = _dot_nt(wgt_ref[...], hb)
    gn_ref[0] = _dot(hb, wgn_ref[...])
    xo_ref[0] = _dot(hb, wxo_ref[...])
    gab_ref[0] = _dot(hb, wgab_ref[...])


def _inproj(x, g_mix, w_in, tm):
    B, T, _ = x.shape
    w = w_in
    wkn = jnp.concatenate([w[:, _O_KCMP:_O_VCMP], w[:, _O_VCMP:_O_KSLC], w[:, _O_KSLC:_O_VSLC],
                           w[:, _O_KWIN:_O_VWIN]], axis=1).astype(BF16)
    wt = jnp.concatenate([w[:, _O_Q:_O_KCMP] * (DH ** -0.5 * LOG2E), w[:, _O_VSLC:_O_KWIN],
                          w[:, _O_VWIN:_O_GNSA]], axis=1).T.astype(BF16)
    wg = jnp.concatenate([w[:, _O_GNSA:_O_XM], w[:, _O_IPRE:_O_GA]], axis=1)
    wgt = wg.T.astype(BF16)
    wgn = jnp.pad(wg, ((0, 0), (0, LANE - 32))).astype(BF16)
    wxo = w[:, _O_XM:_O_IPRE].astype(BF16)
    wgab = w[:, _O_GA:].astype(BF16)
    nt = T // tm
    row = lambda b, i: (b, i, 0)
    col = lambda b, i: (b, 0, i)
    full = lambda b, i: (0, 0)
    out_shape = (
        jax.ShapeDtypeStruct((B, T, 128), BF16), jax.ShapeDtypeStruct((B, T, 128), BF16),
        jax.ShapeDtypeStruct((B, T, 128), BF16), jax.ShapeDtypeStruct((B, T, 128), BF16),
        jax.ShapeDtypeStruct((B, 768, T), BF16), jax.ShapeDtypeStruct((B, 32, T), F32),
        jax.ShapeDtypeStruct((B, T, 128), F32), jax.ShapeDtypeStruct((B, T, 1024), F32),
        jax.ShapeDtypeStruct((B, T, 2048), F32),
    )
    return pl.pallas_call(
        _inproj_kernel,
        grid=(B, nt),
        in_specs=[
            pl.BlockSpec((1, tm, D), row), pl.BlockSpec((1, D), full),
            pl.BlockSpec((D, 512), full), pl.BlockSpec((768, D), full), pl.BlockSpec((32, D), full),
            pl.BlockSpec((D, 128), full), pl.BlockSpec((D, 1024), full), pl.BlockSpec((D, 2048), full),
        ],
        out_specs=(
            pl.BlockSpec((1, tm, 128), row), pl.BlockSpec((1, tm, 128), row),
            pl.BlockSpec((1, tm, 128), row), pl.BlockSpec((1, tm, 128), row),
            pl.BlockSpec((1, 768, tm), col), pl.BlockSpec((1, 32, tm), col),
            pl.BlockSpec((1, tm, 128), row), pl.BlockSpec((1, tm, 1024), row),
            pl.BlockSpec((1, tm, 2048), row),
        ),
        out_shape=out_shape,
        compiler_params=_cparams(("parallel", "parallel")),
        name="inproj",
    )(x, g_mix.reshape(1, D), wkn, wt, wgt, wgn, wxo, wgab)


def _compress_kernel(xk_ref, xv_ref, pek_ref, pev_ref, wklo_ref, wkhi_ref, wk2_ref,
                     wvlo_ref, wvhi_ref, wv2t_ref, kc_ref, vct_ref):
    n_sub = xk_ref.shape[1]

    def hidden(x, pe_ref, wlo_ref, whi_ref):
        wlo = wlo_ref[...]
        whi = whi_ref[...]
        c = _dot(pe_ref[0], wlo) + _dot(pe_ref[1], whi)
        a = _dot(x, wlo)
        b = _dot(x, whi)
        pre = a + pltpu.roll(b, n_sub - 1, 0) + c[0:1, :]
        return jax.nn.gelu(pre).astype(BF16)

    hk = hidden(xk_ref[0], pek_ref, wklo_ref, wkhi_ref)
    kc_ref[0] = _dot(hk, wk2_ref[...]).astype(BF16)
    hv = hidden(xv_ref[0], pev_ref, wvlo_ref, wvhi_ref)
    vct_ref[0] = _dot_nt(wv2t_ref[...], hv).astype(BF16)


def _compress(kcmp, vcmp, pe_k, pe_v, w1k, w2k, w1v, w2v):
    B, T, _ = kcmp.shape
    n_sub = T // CMP_STRIDE
    eye = jnp.eye(G, dtype=F32)

    def big1(w1):
        lo = w1[:CMP_STRIDE * DH].reshape(CMP_STRIDE, DH, PHI)
        hi = w1[CMP_STRIDE * DH:].reshape(CMP_STRIDE, DH, PHI)
        f = lambda a: jnp.einsum('rdp,gh->rgdhp', a, eye).reshape(CMP_STRIDE * G * DH, G * PHI).astype(BF16)
        return f(lo), f(hi)

    def big2(w2):
        return jnp.einsum('pd,gh->gphd', w2, eye).reshape(G * PHI, G * DH)

    def bigpe(pe):
        f = lambda a: jnp.broadcast_to(a[:, None, :], (CMP_STRIDE, G, DH)).reshape(1, CMP_STRIDE * G * DH)
        both = jnp.stack([f(pe[:CMP_STRIDE]), f(pe[CMP_STRIDE:])], axis=0)
        return jnp.broadcast_to(both, (2, SUB, CMP_STRIDE * G * DH)).astype(BF16)

    wklo, wkhi = big1(w1k)
    wvlo, wvhi = big1(w1v)
    wk2 = big2(w2k).astype(BF16)
    wv2t = big2(w2v).T.astype(BF16)
    kw = CMP_STRIDE * G * DH
    xk = kcmp.reshape(B, n_sub, kw)
    xv = vcmp.reshape(B, n_sub, kw)
    c2 = lambda b: (0, 0)
    c3 = lambda b: (0, 0, 0)
    bsel = lambda b: (b, 0, 0)
    return pl.pallas_call(
        _compress_kernel,
        grid=(B,),
        in_specs=[
            pl.BlockSpec((1, n_sub, kw), bsel), pl.BlockSpec((1, n_sub, kw), bsel),
            pl.BlockSpec((2, SUB, kw), c3), pl.BlockSpec((2, SUB, kw), c3),
            pl.BlockSpec((kw, G * PHI), c2), pl.BlockSpec((kw, G * PHI), c2), pl.BlockSpec((G * PHI, G * DH), c2),
            pl.BlockSpec((kw, G * PHI), c2), pl.BlockSpec((kw, G * PHI), c2), pl.BlockSpec((G * DH, G * PHI), c2),
        ],
        out_specs=(pl.BlockSpec((1, n_sub, G * DH), bsel), pl.BlockSpec((1, G * DH, n_sub), bsel)),
        out_shape=(jax.ShapeDtypeStruct((B, n_sub, G * DH), BF16),
                   jax.ShapeDtypeStruct((B, G * DH, n_sub), BF16)),
        compiler_params=_cparams(("parallel",)),
        name="compress",
    )(xk, xv, bigpe(pe_k), bigpe(pe_v), wklo, wkhi, wk2, wvlo, wvhi, wv2t)


def _topk_mask_axis0(score, k):
    n = score.shape[0]
    idx = lax.broadcasted_iota(I32, score.shape, 0)
    work = score
    for _ in range(k):
        m = jnp.max(work, axis=0, keepdims=True)
        first = jnp.min(jnp.where(work == m, idx, n), axis=0, keepdims=True)
        work = jnp.where(idx == first, LOWEST, work)
    return work < 0.5 * LOWEST


def _nsa_kernel(qt_ref, kslc_ref, vslct_ref, kwin_ref, vwint_ref, kc_ref, vct_ref, gt_ref,
                at_ref, e_ref, out_ref, acc_ref, accw_ref, *, tq):
    T = kslc_ref.shape[1]
    n_sub = T // CMP_STRIDE
    nsel = min(NSEL, T // SLC)
    i = pl.program_id(1)
    q0 = i * tq
    t_row = q0 + lax.broadcasted_iota(I32, (1, tq), 1)
    kcv = kc_ref[0]
    cmp_end = lax.broadcasted_iota(I32, (n_sub, 1), 0) * CMP_STRIDE + (2 * CMP_STRIDE - 1)
    bias_c = jnp.where(cmp_end <= t_row, 0.0, NEG)
    col_ok = t_row >= 2 * CMP_STRIDE - 1
    k_loc = lax.broadcasted_iota(I32, (tq, tq), 0)
    q_loc = lax.broadcasted_iota(I32, (tq, tq), 1)
    tri_bias = jnp.where(k_loc <= q_loc, 0.0, NEG)
    prev_bias = jnp.where(k_loc > q_loc + jnp.where(i > 0, 0, tq), 0.0, NEG)
    blk = lax.broadcasted_iota(I32, (NBP, tq), 0)
    cur = t_row // SLC
    at = at_ref[...]
    zeros_half = jnp.zeros((DH, tq), BF16)
    pieces = []
    for g in range(G):
        glo, ghi = g * DH, (g + 1) * DH
        qps = []
        for h in range(HG):
            r0 = (g * HG + h) * DH
            qh = qt_ref[0, r0:r0 + DH, :]
            qps.append(jnp.concatenate([qh, zeros_half] if g == 0 else [zeros_half, qh], axis=0))
        qp_all = jnp.concatenate(qps, axis=1)

        vc_aug = jnp.concatenate([vct_ref[0, glo:ghi, :], jnp.ones((2 * SUB, n_sub), BF16), at], axis=0)
        s_c = _dot(kcv, qp_all)
        imp = jnp.zeros((NBP, tq), F32)
        o_cmp = []
        for h in range(HG):
            s_h = s_c[:, h * tq:(h + 1) * tq] + bias_c
            p = jnp.exp2(s_h - jnp.max(s_h, axis=0, keepdims=True)).astype(BF16)
            r = _dot(vc_aug, p)
            inv = jnp.where(col_ok, 1.0 / r[DH:DH + 1], 0.0)
            o_cmp.append(r[0:DH] * inv)
            imp = imp + r[DH + 2 * SUB:] * inv
        forced = jnp.where(blk == 0, 0.0, jnp.where(blk == cur, 0.0, jnp.where(blk == cur - 1, 0.0, NEG)))
        free = jnp.where(blk <= cur, jnp.where(forced < 0.0, imp, NEG), NEG)
        picked = _topk_mask_axis0(free, nsel - 3)
        selbias = jnp.where(picked, 0.0, forced).astype(BF16)
        q_aug = jnp.concatenate([qp_all, jnp.concatenate([selbias] * HG, axis=1)], axis=0)
        def chunk(k_ref, vt_ref, acc, rhs, c, ms, bias, glo=glo, ghi=ghi, kc=tq):
            k0 = pl.multiple_of(c * kc, kc)
            lhs = k_ref[0, pl.ds(k0, kc), :]
            if rhs.shape[0] == 2 * LANE:
                lhs = jnp.concatenate([lhs, e_ref[pl.ds(k0, kc), :]], axis=1)
            v_aug = jnp.concatenate([vt_ref[0, glo:ghi, pl.ds(k0, kc)], jnp.ones((2 * SUB, kc), BF16)], axis=0)
            s = _dot(lhs, rhs)
            new_ms = []
            for h in range(HG):
                s_h = s[:, h * tq:(h + 1) * tq]
                if bias is not None:
                    s_h = s_h + bias
                m_new = jnp.maximum(ms[h], jnp.max(s_h, axis=0, keepdims=True))
                alpha = jnp.exp2(ms[h] - m_new)
                p = jnp.exp2(s_h - m_new).astype(BF16)
                acc[h] = alpha * acc[h] + _dot(v_aug, p)
                new_ms.append(m_new)
            return tuple(new_ms)

        ms0 = tuple(jnp.full((1, tq), NEG, F32) for _ in range(HG))
        acc_ref[...] = jnp.zeros(acc_ref.shape, F32)
        ms = lax.fori_loop(0, i // 2, lambda c, ms, q_aug=q_aug: chunk(kslc_ref, vslct_ref, acc_ref, q_aug, c, ms,
                                                                       None, kc=2 * tq), ms0)
        ms = lax.fori_loop(2 * (i // 2), i,
                           lambda c, ms, q_aug=q_aug: chunk(kslc_ref, vslct_ref, acc_ref, q_aug, c, ms, None), ms)
        chunk(kslc_ref, vslct_ref, acc_ref, q_aug, i, ms, tri_bias)
        accw_ref[...] = jnp.zeros(accw_ref.shape, F32)
        ms = chunk(kwin_ref, vwint_ref, accw_ref, qp_all, i, ms0, tri_bias)
        chunk(kwin_ref, vwint_ref, accw_ref, qp_all, jnp.maximum(i - 1, 0), ms, prev_bias)
        for h in range(HG):
            accv = acc_ref[h]
            o_slc = accv[0:DH] * (1.0 / accv[DH:DH + 1])
            accv = accw_ref[h]
            o_win = accv[0:DH] * (1.0 / accv[DH:DH + 1])
            gi = (g * HG + h) * 3
            sg = jax.nn.sigmoid(gt_ref[0, gi:gi + 3, :])
            pieces.append(sg[0:1] * o_cmp[h] + sg[1:2] * o_slc + sg[2:3] * o_win)
    yt = jnp.concatenate(pieces, axis=0)
    out_ref[0] = yt.T.astype(BF16)


def _nsa(tt, kslc, kwin, kc, vct, gt, tq):
    B, T, _ = kslc.shape
    n_sub = T // CMP_STRIDE
    nb = T // SLC
    assert nb <= NBP and T % tq == 0
    assert tq == WIN
    cidx = jnp.arange(n_sub)[None, :]
    bidx = jnp.arange(NBP)[:, None]
    off = cidx - (SLC // CMP_STRIDE) * bidx
    at = (jnp.where((off >= -1) & (off <= 3), 1.0, 0.0)
          + jnp.where((off >= 0) & (off <= 2), 1.0, 0.0))
    at = jnp.where((cidx < n_sub - 1) & (bidx < nb), at, 0.0).astype(BF16)
    e = (jnp.arange(T)[:, None] // SLC == jnp.arange(NBP)[None, :]).astype(BF16)
    nt = T // tq
    kern = functools.partial(_nsa_kernel, tq=tq)
    bfull = lambda b, i: (b, 0, 0)
    return pl.pallas_call(
        kern,
        grid=(B, nt),
        in_specs=[
            pl.BlockSpec((1, 512, tq), lambda b, i: (b, 0, i)),
            pl.BlockSpec((1, T, 128), bfull),
            pl.BlockSpec((1, 128, T), lambda b, i: (b, 4, 0)),
            pl.BlockSpec((1, T, 128), bfull),
            pl.BlockSpec((1, 128, T), lambda b, i: (b, 5, 0)),
            pl.BlockSpec((1, n_sub, 128), bfull),
            pl.BlockSpec((1, 128, n_sub), bfull),
            pl.BlockSpec((1, 32, tq), lambda b, i: (b, 0, i)),
            pl.BlockSpec((NBP, n_sub), lambda b, i: (0, 0)),
            pl.BlockSpec((T, NBP), lambda b, i: (0, 0)),
        ],
        out_specs=pl.BlockSpec((1, tq, 512), lambda b, i: (b, i, 0)),
        out_shape=jax.ShapeDtypeStruct((B, T, 512), BF16),
        scratch_shapes=[pltpu.VMEM((HG, DH + 2 * SUB, tq), F32), pltpu.VMEM((HG, DH + 2 * SUB, tq), F32)],
        compiler_params=_cparams(("parallel", "parallel")),
        name="nsa",
    )(tt, kslc, tt, kwin, tt, kc, vct, gt, at, e)


def _log_sigmoid(x):
    return jnp.minimum(x, 0.0) - jnp.log(1.0 + jnp.exp(-jnp.abs(x)))


def _mlstm_kernel(xo_ref, gn_ref, gt_ref, bn_ref, bt_ref, wconv_ref, bconv_ref, wq_ref, wk_ref, wv_ref,
                  gain_ref, skip_ref, tri_ref, trit_ref, out_ref, xbuf, caug, mst, *, L):
    c = pl.program_id(1)

    @pl.when(c == 0)
    def _():
        xbuf[...] = jnp.zeros(xbuf.shape, F32)
        caug[...] = jnp.zeros(caug.shape, F32)
        mst[...] = jnp.zeros(mst.shape, F32)

    xbuf[0:SUB, :] = xbuf[L:L + SUB, :]
    xbuf[SUB:SUB + L, :] = xo_ref[0, :, 0:MI]
    conv = bconv_ref[...] + wconv_ref[3:4, :] * xbuf[SUB:SUB + L, :]
    for r in range(3):
        s = 3 - r
        conv = conv + wconv_ref[r:r + 1, :] * xbuf[SUB - s:SUB - s + L, :]
    xc = conv * jax.nn.sigmoid(conv)

    gn = gn_ref[0] + bn_ref[...]
    gt = gt_ref[0] + bt_ref[...]
    lf_hi, lf_lo = _split(_log_sigmoid(gn))
    tri = tri_ref[...]
    b_col_all = _dot(tri, lf_hi) + _dot(tri, lf_lo)
    lft_hi, lft_lo = _split(_log_sigmoid(gt))
    trit = trit_ref[...]
    b_row_all = _dot(lft_hi, trit) + _dot(lft_lo, trit)
    ti = lax.broadcasted_iota(I32, (L, L), 0)
    si = lax.broadcasted_iota(I32, (L, L), 1)
    causal = si <= ti
    one_col = jnp.where(lax.broadcasted_iota(I32, (L, DHB), 1) == 0, 1.0, 0.0).astype(BF16)

    outs = []
    for h in range(HB):
        lo, hi = h * DHB, (h + 1) * DHB
        xch = xc[:, lo:hi]
        xcb = xch.astype(BF16)
        q = _dot(xcb, wq_ref[h]).astype(BF16)
        k = _dot(xcb, wk_ref[h]) * (DHB ** -0.5)
        v = _dot(xo_ref[0, :, lo:hi].astype(BF16), wv_ref[h]).astype(BF16)
        v_aug = jnp.concatenate([v, one_col], axis=1)
        bc = b_col_all[:, 28 + h:29 + h]
        lic = gn[:, 24 + h:25 + h]
        br = b_row_all[28 + h:29 + h, :]
        lir = gt[24 + h:25 + h, :]
        m_prev = mst[h, 0:1, 0:1]
        log_d = jnp.where(causal, bc - br + lir, NEG)
        m_inter = bc + m_prev
        m_t = jnp.maximum(jnp.max(log_d, axis=1, keepdims=True), m_inter)
        dmat = jnp.exp(log_d - m_t)
        s = (_dot_nt(q, k.astype(BF16)) * dmat).astype(BF16)
        inter = jnp.exp(m_inter - m_t)
        c_old = caug[h]
        r = _dot(s, v_aug) + inter * _dot(q, c_old.astype(BF16))
        num = r[:, 0:DHB]
        den = r[:, DHB:DHB + 1]
        hval = num / jnp.maximum(jnp.abs(den), jnp.exp(-m_t))
        b_last = bc[L - 1:L, :]
        log_w = b_last - bc + lic
        m_new = jnp.maximum(b_last + m_prev, jnp.max(log_w, axis=0, keepdims=True))
        w = jnp.exp(log_w - m_new)
        decay = jnp.exp(b_last + m_prev - m_new)
        kwt = (k * w).T.astype(BF16)
        caug[h] = decay * c_old + _dot(kwt, v_aug)
        mst[h] = jnp.broadcast_to(m_new, (SUB, LANE))
        o = jax.nn.sigmoid(xo_ref[0, :, MI + lo:MI + hi]) * hval
        mu = jnp.mean(o, axis=-1, keepdims=True)
        var = jnp.mean((o - mu) ** 2, axis=-1, keepdims=True)
        hn = (o - mu) * lax.rsqrt(var + EPS) * gain_ref[:, lo:hi]
        outs.append(hn + skip_ref[:, lo:hi] * xch)
    out_ref[0] = jnp.concatenate(outs, axis=1).astype(BF16)


def _mlstm(xo, gn, gt, w_conv, b_conv, w_q, w_k, w_v, b_i, b_f, mh_gain, skip, L):
    B, T, _ = xo.shape
    bias = jnp.concatenate([jnp.zeros((24,), F32), b_i, b_f])
    bn = jnp.pad(bias, (0, LANE - 32)).reshape(1, LANE)
    bt = bias.reshape(32, 1)
    tri = (jnp.arange(L)[None, :] <= jnp.arange(L)[:, None]).astype(BF16)
    c2 = lambda b, c: (0, 0)
    c3 = lambda b, c: (0, 0, 0)
    return pl.pallas_call(
        functools.partial(_mlstm_kernel, L=L),
        grid=(B, T // L),
        in_specs=[
            pl.BlockSpec((1, L, 2 * MI), lambda b, c: (b, c, 0)),
            pl.BlockSpec((1, L, LANE), lambda b, c: (b, c, 0)),
            pl.BlockSpec((1, 32, L), lambda b, c: (b, 0, c)),
            pl.BlockSpec((1, LANE), c2), pl.BlockSpec((32, 1), c2),
            pl.BlockSpec((4, MI), c2), pl.BlockSpec((1, MI), c2),
            pl.BlockSpec((HB, DHB, DHB), c3), pl.BlockSpec((HB, DHB, DHB), c3), pl.BlockSpec((HB, DHB, DHB), c3),
            pl.BlockSpec((1, MI), c2), pl.BlockSpec((1, MI), c2),
            pl.BlockSpec((L, L), c2), pl.BlockSpec((L, L), c2),
        ],
        out_specs=pl.BlockSpec((1, L, MI), lambda b, c: (b, c, 0)),
        out_shape=jax.ShapeDtypeStruct((B, T, MI), BF16),
        scratch_shapes=[pltpu.VMEM((L + 2 * SUB, MI), F32), pltpu.VMEM((HB, DHB, 2 * DHB), F32),
                        pltpu.VMEM((HB, SUB, LANE), F32)],
        compiler_params=_cparams(("parallel", "arbitrary")),
        name="mlstm",
    )(xo, gn, gt, bn, bt, w_conv, b_conv.reshape(1, MI), w_q.astype(BF16), w_k.astype(BF16),
      w_v.astype(BF16), mh_gain.reshape(1, MI), skip.reshape(1, MI), tri, tri.T)


def _memkv_kernel(m_ref, g_ref, wk_ref, wv_ref, k_ref, v_ref):
    hb = _rms(m_ref[0], g_ref[...]).astype(BF16)
    k_ref[0] = _dot(hb, wk_ref[...]).astype(BF16)
    v_ref[0] = _dot(hb, wv_ref[...]).astype(BF16)


def _memkv(mem, g_mem, w_k, w_v):
    B, M, _ = mem.shape
    c2 = lambda b: (0, 0)
    bs = lambda b: (b, 0, 0)
    return pl.pallas_call(
        _memkv_kernel,
        grid=(B,),
        in_specs=[pl.BlockSpec((1, M, D), bs), pl.BlockSpec((1, D), c2),
                  pl.BlockSpec((D, D), c2), pl.BlockSpec((D, D), c2)],
        out_specs=(pl.BlockSpec((1, M, D), bs), pl.BlockSpec((1, M, D), bs)),
        out_shape=(jax.ShapeDtypeStruct((B, M, D), BF16), jax.ShapeDtypeStruct((B, M, D), BF16)),
        compiler_params=_cparams(("parallel",)),
        name="memkv",
    )(mem, g_mem.reshape(1, D), w_k.astype(BF16), w_v.astype(BF16))


def _mid_kernel(x_ref, ya_ref, yb_ref, gab_ref, wua_ref, wub_ref, wo_ref, gx_ref, wq_ref, kx_ref, vx_ref,
                wox_ref, gf_ref, wrh_ref, wrl_ref, br_ref,
                x2_ref, h3_ref, ti_ref, tw_ref, cnt_ref, *, tm):
    a = _dot(ya_ref[0], wua_ref[...])
    b = _dot(yb_ref[0], wub_ref[...])
    merged = jax.nn.sigmoid(gab_ref[0, :, 0:D]) * a + jax.nn.sigmoid(gab_ref[0, :, D:2 * D]) * b
    x1 = x_ref[0] + _dot(merged.astype(BF16), wo_ref[...])
    q = (_dot(_rms(x1, gx_ref[...]).astype(BF16), wq_ref[...]) * (DHX ** -0.5)).astype(BF16)
    kx = kx_ref[0]
    vx = vx_ref[0]
    heads = []
    for h in range(HX):
        lo, hi = h * DHX, (h + 1) * DHX
        s = _dot_nt(q[:, lo:hi], kx[:, lo:hi])
        m = jnp.max(s, axis=-1, keepdims=True)
        e = jnp.exp(s - m)
        p = e * (1.0 / jnp.sum(e, axis=-1, keepdims=True))
        heads.append(_dot(p.astype(BF16), vx[:, lo:hi]).astype(BF16))
    x2 = x1 + _dot(jnp.concatenate(heads, axis=1), wox_ref[...])
    x2_ref[0] = x2
    h3 = _rms(x2, gf_ref[...])
    h3_ref[0] = h3
    h_hi, h_lo = _split(h3)
    logits = (_dot(h_hi, wrh_ref[...]) + _dot(h_lo, wrh_ref[...]) + _dot(h_hi, wrl_ref[...])) + br_ref[...]
    lane = lax.broadcasted_iota(I32, (tm, LANE), 1)
    work = logits
    vals, idxs = [], []
    multi = jnp.zeros((tm, LANE), F32)
    for _ in range(TOPK):
        m = jnp.max(work, axis=-1, keepdims=True)
        first = jnp.min(jnp.where(work == m, lane, LANE), axis=-1, keepdims=True)
        pick = lane == first
        vals.append(m)
        idxs.append(first)
        multi = jnp.where(pick, 1.0, multi)
        work = jnp.where(pick, LOWEST, work)
    es = [jnp.exp(v - vals[0]) for v in vals]
    inv = 1.0 / (es[0] + es[1] + es[2] + es[3])
    tw = jnp.zeros((tm, LANE), F32)
    for k in range(TOPK):
        tw = jnp.where(lane == k, es[k] * inv, tw)
    tw_ref[0] = tw
    ti = jnp.zeros((tm, LANE), I32)
    for k in range(TOPK):
        ti = jnp.where(lane == k, idxs[k], ti)
    ti_ref[0] = ti
    cnt_ref[0] = jnp.broadcast_to(jnp.sum(multi, axis=0, keepdims=True), (SUB, LANE))


def _mid(x, ya, yb, gab, w_up_a, w_up_b, w_o_mix, g_x, w_q_x, kx, vx, w_o_x, g_ffn, w_router, b_router, tm):
    B, T, _ = x.shape
    M = kx.shape[1]
    wr = jnp.pad(w_router, ((0, 0), (0, LANE - NE)))
    wrh, wrl = _split(wr)
    br = jnp.concatenate([b_router, jnp.full((LANE - NE,), NEG, F32)]).reshape(1, LANE)
    c2 = lambda b, i: (0, 0)
    row = lambda b, i: (b, i, 0)
    bs = lambda b, i: (b, 0, 0)
    nt = T // tm
    return pl.pallas_call(
        functools.partial(_mid_kernel, tm=tm),
        grid=(B, nt),
        in_specs=[
            pl.BlockSpec((1, tm, D), row), pl.BlockSpec((1, tm, 512), row), pl.BlockSpec((1, tm, MI), row),
            pl.BlockSpec((1, tm, 2 * D), row),
            pl.BlockSpec((512, D), c2), pl.BlockSpec((MI, D), c2), pl.BlockSpec((D, D), c2),
            pl.BlockSpec((1, D), c2), pl.BlockSpec((D, D), c2),
            pl.BlockSpec((1, M, D), bs), pl.BlockSpec((1, M, D), bs),
            pl.BlockSpec((D, D), c2), pl.BlockSpec((1, D), c2),
            pl.BlockSpec((D, LANE), c2), pl.BlockSpec((D, LANE), c2), pl.BlockSpec((1, LANE), c2),
        ],
        out_specs=(
            pl.BlockSpec((1, tm, D), row), pl.BlockSpec((1, tm, D), row),
            pl.BlockSpec((1, tm, LANE), row), pl.BlockSpec((1, tm, LANE), row),
            pl.BlockSpec((1, SUB, LANE), lambda b, i: (b * nt + i, 0, 0)),
        ),
        out_shape=(
            jax.ShapeDtypeStruct((B, T, D), F32), jax.ShapeDtypeStruct((B, T, D), F32),
            jax.ShapeDtypeStruct((B, T, LANE), I32), jax.ShapeDtypeStruct((B, T, LANE), F32),
            jax.ShapeDtypeStruct((B * nt, SUB, LANE), F32),
        ),
        compiler_params=_cparams(("parallel", "parallel")),
        name="mid",
    )(x, ya, yb, gab, w_up_a.astype(BF16), w_up_b.astype(BF16), w_o_mix.astype(BF16), g_x.reshape(1, D),
      w_q_x.astype(BF16), kx, vx, w_o_x.astype(BF16), g_ffn.reshape(1, D), wrh, wrl, br)


def _local_capacity(tm):
    return TOPK * tm + NE * SUB


def _local_rows(ti, ltri, utri):
    tm = ti.shape[0]
    lane = lax.broadcasted_iota(I32, (tm, LANE), 1)
    hots = [lane == ti[:, k:k + 1] for k in range(TOPK)]
    multi = jnp.zeros((tm, LANE), F32)
    for hot in hots:
        multi = jnp.where(hot, 1.0, multi)
    prior = _dot(ltri, multi.astype(BF16))
    cnt = jnp.broadcast_to(jnp.sum(multi, axis=0, keepdims=True), (SUB, LANE))
    cnt = jnp.ceil(cnt * (1.0 / SUB)) * SUB
    loc = _dot(cnt.astype(BF16), utri)[0:1]
    base = prior + loc
    return [jnp.sum(jnp.where(hot, base, 0.0), axis=-1, keepdims=True) for hot in hots]


def _run_copies(cnt_ref, gs_ref, tile, make_copy):
    def body(e, local):
        n = pl.multiple_of(cnt_ref[tile * NE + e], SUB)

        @pl.when(n > 0)
        def _():
            make_copy(pl.multiple_of(local, SUB), pl.multiple_of(gs_ref[tile * NE + e], SUB), n).start()

        return local + n

    lax.fori_loop(0, NE, body, 0)


def _dispatch_kernel(cnt_ref, gs_ref, rows_ref, ps_ref, pn_ref, na_ref, h3_ref, ti_ref, ltri_ref, utri_ref,
                     xbuf_hbm, xl, zbuf, sem, zsem, *, tm, nt, n_blk):
    j = pl.program_id(0)
    slot = j % 2
    rows = xl.shape[1]

    def wait_tile(s, tile):
        n = pl.multiple_of(rows_ref[tile], SUB)
        pltpu.make_async_copy(xl.at[s, pl.ds(0, n), :], xbuf_hbm.at[pl.ds(0, n), :], sem.at[s]).wait()

    def pad_copy(e):
        n = pl.multiple_of(pn_ref[e], SUB)
        return pltpu.make_async_copy(zbuf.at[pl.ds(0, n), :],
                                     xbuf_hbm.at[pl.ds(pl.multiple_of(ps_ref[e], SUB), n), :], zsem)

    def tail_copy(b):
        return pltpu.make_async_copy(zbuf, xbuf_hbm.at[pl.ds(pl.multiple_of(b * MBLK, MBLK), MBLK), :], zsem)

    def for_tail_blocks(fn):
        def body(b, carry):
            fn(b)
            return carry

        lax.fori_loop(na_ref[0], n_blk, body, 0)

    @pl.when(j >= 2)
    def _():
        wait_tile(slot, j - 2)

    @pl.when(j == 0)
    def _():
        zbuf[...] = jnp.zeros(zbuf.shape, F32)
        for e in range(NE):
            @pl.when(pn_ref[e] > 0)
            def _(e=e):
                pad_copy(e).start()
        for_tail_blocks(lambda b: tail_copy(b).start())

    lrs = _local_rows(ti_ref[...], ltri_ref[...], utri_ref[...])
    lane = lax.broadcasted_iota(I32, (tm, LANE), 1)
    lmat = jnp.zeros((tm, LANE), F32)
    for k in range(TOPK):
        lmat = jnp.where(lane == k, lrs[k], lmat)
    lt = lmat.T
    rio = lax.broadcasted_iota(I32, (rows, 1), 0).astype(F32)
    st = jnp.zeros((rows, tm), F32)
    for k in range(TOPK):
        st = jnp.where(rio == lt[k:k + 1, :], 1.0, st)
    xl[slot] = _dot(st.astype(BF16), h3_ref[...].astype(BF16))
    _run_copies(cnt_ref, gs_ref, j,
                lambda l, g, n: pltpu.make_async_copy(xl.at[slot, pl.ds(l, n), :], xbuf_hbm.at[pl.ds(g, n), :],
                                                      sem.at[slot]))

    @pl.when(j == nt - 1)
    def _():
        wait_tile(slot, j)
        if nt >= 2:
            wait_tile(1 - slot, j - 1)
        for e in range(NE):
            @pl.when(pn_ref[e] > 0)
            def _(e=e):
                pad_copy(e).wait()
        for_tail_blocks(lambda b: tail_copy(b).wait())


def _dispatch(h3, ti, plan, n_slots, tm):
    n_tok = h3.shape[0]
    nt = n_tok // tm
    ltri = (jnp.arange(tm)[None, :] < jnp.arange(tm)[:, None]).astype(BF16)
    utri = (jnp.arange(LANE)[:, None] < jnp.arange(LANE)[None, :]).astype(BF16)
    c2 = lambda j, *_: (0, 0)
    grid_spec = pltpu.PrefetchScalarGridSpec(
        num_scalar_prefetch=6,
        grid=(nt,),
        in_specs=[
            pl.BlockSpec((tm, D), lambda j, *_: (j, 0)), pl.BlockSpec((tm, LANE), lambda j, *_: (j, 0)),
            pl.BlockSpec((tm, tm), c2), pl.BlockSpec((LANE, LANE), c2),
        ],
        out_specs=pl.BlockSpec(memory_space=pl.ANY),
        scratch_shapes=[pltpu.VMEM((2, _local_capacity(tm), D), F32), pltpu.VMEM((MBLK, D), F32),
                        pltpu.SemaphoreType.DMA((2,)), pltpu.SemaphoreType.DMA(())],
    )
    return pl.pallas_call(
        functools.partial(_dispatch_kernel, tm=tm, nt=nt, n_blk=n_slots // MBLK),
        grid_spec=grid_spec,
        out_shape=jax.ShapeDtypeStruct((n_slots, D), F32),
        compiler_params=_cparams(("arbitrary",)),
        name="dispatch",
    )(plan["cnt"], plan["gstart"], plan["rows"], plan["pad_start"], plan["pad_len"], plan["nact"],
      h3, ti, ltri, utri)


def _expert_kernel(be_ref, na_ref, nx_ref, x_ref, wgu_hbm, bgu_ref, wdn_hbm, bdn_ref, out_ref,
                   wgu_f, wdn_f, wgu_bf, wdn_bf, sem):
    i = pl.program_id(0)
    nact = na_ref[0]

    def weight_copies(e):
        return (pltpu.make_async_copy(wgu_hbm.at[e], wgu_f, sem.at[0]),
                pltpu.make_async_copy(wdn_hbm.at[e], wdn_f, sem.at[1]))

    @pl.when(i == 0)
    def _():
        for c in weight_copies(be_ref[0]):
            c.start()

    @pl.when(i < nact)
    def _():
        @pl.when(jnp.logical_or(i == 0, be_ref[i] != be_ref[jnp.maximum(i - 1, 0)]))
        def _():
            for c in weight_copies(be_ref[i]):
                c.wait()
            wgu_bf[...] = wgu_f[...].astype(BF16)
            wdn_bf[...] = wdn_f[...].astype(BF16)
            nxt = nx_ref[be_ref[i]]

            @pl.when(nxt >= 0)
            def _():
                for c in weight_copies(nxt):
                    c.start()

        gu = _dot(x_ref[...].astype(BF16), wgu_bf[...]) + bgu_ref[0]
        g = jnp.minimum(gu[:, 0:DFF], LIMIT)
        u = jnp.clip(gu[:, DFF:2 * DFF], -LIMIT, LIMIT)
        act = (u + 1.0) * (g * jax.nn.sigmoid(ALPHA * g))
        out_ref[...] = _dot(act.astype(BF16), wdn_bf[...]) + bdn_ref[0]

    @pl.when(i >= nact)
    def _():
        out_ref[...] = jnp.zeros(out_ref.shape, F32)


def _experts(xbuf, plan, w_gu, b_gu, w_down, b_down):
    n_blk = plan["blk_exp"].shape[0]
    wsel = lambda i, be, na, nx: (be[i], 0, 0)
    grid_spec = pltpu.PrefetchScalarGridSpec(
        num_scalar_prefetch=3,
        grid=(n_blk,),
        in_specs=[
            pl.BlockSpec((MBLK, D), lambda i, be, na, nx: (jnp.minimum(i, na[0] - 1), 0)),
            pl.BlockSpec(memory_space=pl.ANY), pl.BlockSpec((1, 1, 2 * DFF), wsel),
            pl.BlockSpec(memory_space=pl.ANY), pl.BlockSpec((1, 1, D), wsel),
        ],
        out_specs=pl.BlockSpec((MBLK, D), lambda i, be, na, nx: (i, 0)),
        scratch_shapes=[pltpu.VMEM((D, 2 * DFF), F32), pltpu.VMEM((DFF, D), F32),
                        pltpu.VMEM((D, 2 * DFF), BF16), pltpu.VMEM((DFF, D), BF16),
                        pltpu.SemaphoreType.DMA((2,))],
    )
    return pl.pallas_call(
        _expert_kernel,
        grid_spec=grid_spec,
        out_shape=jax.ShapeDtypeStruct((n_blk * MBLK, D), F32),
        compiler_params=_cparams(("arbitrary",)),
        name="experts",
    )(plan["blk_exp"], plan["nact"], plan["next_expert"], xbuf, w_gu, b_gu.reshape(NE, 1, 2 * DFF),
      w_down, b_down.reshape(NE, 1, D))


def _combine_kernel(cnt_ref, gs_ref, rows_ref, x2_ref, ti_ref, tw_ref, gfin_ref, ltri_ref, utri_ref, y_hbm,
                    out_ref, yl, sem, *, tm, nt, final_norm):
    j = pl.program_id(0)
    slot = j % 2
    rows = yl.shape[1]

    def issue(tile, s):
        _run_copies(cnt_ref, gs_ref, tile,
                    lambda l, g, n: pltpu.make_async_copy(y_hbm.at[pl.ds(g, n), :], yl.at[s, pl.ds(l, n), :],
                                                          sem.at[s]))

    @pl.when(j == 0)
    def _():
        yl[...] = jnp.zeros(yl.shape, F32)
        issue(0, 0)

    n_rows = pl.multiple_of(rows_ref[j], SUB)
    pltpu.make_async_copy(y_hbm.at[pl.ds(0, n_rows), :], yl.at[slot, pl.ds(0, n_rows), :], sem.at[slot]).wait()

    @pl.when(j + 1 < nt)
    def _():
        issue(j + 1, 1 - slot)

    lrs = _local_rows(ti_ref[...], ltri_ref[...], utri_ref[...])
    rio = lax.broadcasted_iota(I32, (1, rows), 1).astype(F32)
    sw = jnp.zeros((tm, rows), F32)
    for k in range(TOPK):
        sw = jnp.where(rio == lrs[k], tw_ref[:, k:k + 1], sw)
    yv = yl[slot].astype(BF16)
    s_hi, s_lo = _split(sw)
    acc = x2_ref[...] + (_dot(s_hi, yv) + _dot(s_lo, yv))
    if final_norm:
        acc = _rms(acc, gfin_ref[...])
    out_ref[...] = acc


def _combine(plan, x2, ti, tw, g_final, ybuf, tm, final_norm):
    n_tok = x2.shape[0]
    nt = n_tok // tm
    ltri = (jnp.arange(tm)[None, :] < jnp.arange(tm)[:, None]).astype(BF16)
    utri = (jnp.arange(LANE)[:, None] < jnp.arange(LANE)[None, :]).astype(BF16)
    c2 = lambda j, *_: (0, 0)
    tile = lambda j, *_: (j, 0)
    grid_spec = pltpu.PrefetchScalarGridSpec(
        num_scalar_prefetch=3,
        grid=(nt,),
        in_specs=[
            pl.BlockSpec((tm, D), tile), pl.BlockSpec((tm, LANE), tile), pl.BlockSpec((tm, LANE), tile),
            pl.BlockSpec((1, D), c2), pl.BlockSpec((tm, tm), c2), pl.BlockSpec((LANE, LANE), c2),
            pl.BlockSpec(memory_space=pl.ANY),
        ],
        out_specs=pl.BlockSpec((tm, D), tile),
        scratch_shapes=[pltpu.VMEM((2, _local_capacity(tm), D), F32), pltpu.SemaphoreType.DMA((2,))],
    )
    return pl.pallas_call(
        functools.partial(_combine_kernel, tm=tm, nt=nt, final_norm=final_norm),
        grid_spec=grid_spec,
        out_shape=jax.ShapeDtypeStruct((n_tok, D), F32),
        compiler_params=_cparams(("arbitrary",)),
        name="combine",
    )(plan["cnt"], plan["gstart"], plan["rows"], x2, ti, tw, g_final.reshape(1, D), ltri, utri, ybuf)


def _moe_plan(tile_cnt, n_tok):
    n_tiles = tile_cnt.shape[0]
    cnt = (tile_cnt[:, 0, 0:NE].astype(I32) + SUB - 1) // SUB * SUB
    before = jnp.cumsum(cnt, axis=0) - cnt
    total = jnp.sum(cnt, axis=0)
    padded = (total + MBLK - 1) // MBLK * MBLK
    p_end = jnp.cumsum(padded)
    p_start = p_end - padded
    n_blk = (n_tok * TOPK + n_tiles * NE * (SUB - 1) + MBLK - 1) // MBLK + NE
    blk_start = jnp.arange(n_blk, dtype=I32) * MBLK
    used = jnp.where(padded > 0, jnp.arange(NE, dtype=I32), NE)
    nearest = lax.cummin(used, axis=0, reverse=True)
    next_expert = jnp.concatenate([nearest[1:], jnp.full((1,), NE, I32)])
    return {
        "next_expert": jnp.where(next_expert < NE, next_expert, -1).astype(I32),
        "cnt": cnt.reshape(-1),
        "rows": jnp.sum(cnt, axis=1).astype(I32),
        "gstart": (p_start[None, :] + before).reshape(-1).astype(I32),
        "pad_start": (p_start + total).astype(I32),
        "pad_len": (padded - total).astype(I32),
        "blk_exp": jnp.minimum(jnp.sum(p_end[None, :] <= blk_start[:, None], axis=1), NE - 1).astype(I32),
        "nact": (p_end[-1] // MBLK).astype(I32).reshape(1),
        "n_slots": n_blk * MBLK,
    }


def _layer(x, mem, g_mix, w_in, pe_cmp_k, pe_cmp_v, w_phi1_k, w_phi2_k, w_phi1_v, w_phi2_v,
           w_conv, b_conv, w_q_m, w_k_m, w_v_m, b_i, b_f, mh_gain, skip_m,
           w_up_a, w_up_b, w_o_mix, g_x, g_mem, w_q_x, w_k_x, w_v_x, w_o_x,
           g_ffn, w_router, b_router, w_gu, b_gu, w_down, b_down, g_final, final_norm):
    B, T, _ = x.shape
    kcmp, vcmp, kslc, kwin, tt, gt, gn, xo, gab = _inproj(x, g_mix, w_in, tm=256)
    kc, vct = _compress(kcmp, vcmp, pe_cmp_k, pe_cmp_v, w_phi1_k, w_phi2_k, w_phi1_v, w_phi2_v)
    ya = _nsa(tt, kslc, kwin, kc, vct, gt, tq=WIN)
    yb = _mlstm(xo, gn, gt, w_conv, b_conv, w_q_m, w_k_m, w_v_m, b_i, b_f, mh_gain, skip_m, L=256)
    kx, vx = _memkv(mem, g_mem, w_k_x, w_v_x)
    x2, h3, ti, tw, cnt = _mid(x, ya, yb, gab, w_up_a, w_up_b, w_o_mix, g_x, w_q_x, kx, vx, w_o_x,
                               g_ffn, w_router, b_router, tm=256)
    n_tok = B * T
    plan = _moe_plan(cnt, n_tok)
    ti = ti.reshape(n_tok, LANE)
    xbuf = _dispatch(h3.reshape(n_tok, D), ti, plan, plan["n_slots"], tm=256)
    ybuf = _experts(xbuf, plan, w_gu, b_gu, w_down, b_down)
    out = _combine(plan, x2.reshape(n_tok, D), ti, tw.reshape(n_tok, LANE), g_final, ybuf, tm=256,
                   final_norm=final_norm)
    return out.reshape(B, T, D)


def kernel(x, mem, g_mix, w_in, pe_cmp_k, pe_cmp_v, w_phi1_k, w_phi2_k, w_phi1_v, w_phi2_v, w_conv, b_conv, w_q_m, w_k_m, w_v_m, b_i, b_f, mh_gain, skip_m, w_up_a, w_up_b, w_o_mix, g_x, g_mem, w_q_x, w_k_x, w_v_x, w_o_x, g_ffn, w_router, b_router, w_gu, b_gu, w_down, b_down, g_final):
    layers = (g_mix, w_in, pe_cmp_k, pe_cmp_v, w_phi1_k, w_phi2_k, w_phi1_v, w_phi2_v, w_conv, b_conv,
              w_q_m, w_k_m, w_v_m, b_i, b_f, mh_gain, skip_m, w_up_a, w_up_b, w_o_mix, g_x, g_mem,
              w_q_x, w_k_x, w_v_x, w_o_x, g_ffn, w_router, b_router, w_gu, b_gu, w_down, b_down)
    depth = g_mix.shape[0]
    for l in range(depth):
        x = _layer(x, mem, *(w[l] for w in layers), g_final, final_norm=(l == depth - 1))
    return x
```
